```python
import jax, jax.numpy as jnp
from jax import lax
import numpy as np

D_MODEL = 2048
BATCH = 2
SEQ = 8192
DEPTH = 1
DEC_BATCH = 128
DEC_SEQ = 4
PAST_LEN = 16384
PAGE_SIZE = 128

HEAD_DIM = 64
N_HEADS = D_MODEL // (2 * HEAD_DIM)
N_KV_HEADS = 4
GROUP = N_HEADS // N_KV_HEADS
ATT_W = N_HEADS * HEAD_DIM
KV_W = N_KV_HEADS * HEAD_DIM
WINDOW = 128
BLOCK = WINDOW
D_CONV = D_MODEL // 2
CONV_W = 31
N_BRANCH = 2
D_FF = 4 * D_MODEL
IN_W = ATT_W + 2 * KV_W + 2 * D_CONV + N_BRANCH * D_MODEL
EPS = 1e-6
NEG = -1e30
ATTN_SCALE = HEAD_DIM ** -0.5

kernel_name = "hybrid_swa_sink_conformer_gated_decode_step"


def _rmsnorm(x, g):
    xf = x.astype(jnp.float32)
    y = xf * lax.rsqrt(jnp.mean(xf * xf, axis=-1, keepdims=True) + EPS)
    return (y * g.astype(jnp.float32)).astype(x.dtype)


def _layernorm(x, g, b):
    xf = x.astype(jnp.float32)
    mu = jnp.mean(xf, axis=-1, keepdims=True)
    xc = xf - mu
    y = xc * lax.rsqrt(jnp.mean(xc * xc, axis=-1, keepdims=True) + EPS)
    return (y * g.astype(jnp.float32) + b.astype(jnp.float32)).astype(x.dtype)


def _project(h, w_in, b_gate):
    lead = h.shape[:-1]
    p = h @ w_in
    q, k, v, u, gl = jnp.split(
        p, [ATT_W, ATT_W + KV_W, ATT_W + 2 * KV_W, ATT_W + 2 * KV_W + 2 * D_CONV], axis=-1)
    q = q.reshape(*lead, N_KV_HEADS, GROUP, HEAD_DIM)
    k = k.reshape(*lead, N_KV_HEADS, HEAD_DIM)
    v = v.reshape(*lead, N_KV_HEADS, HEAD_DIM)
    ua, ub = jnp.split(u, 2, axis=-1)
    u = ua * jax.nn.sigmoid(ub)
    gates = jax.nn.sigmoid(gl + b_gate).reshape(*lead, N_BRANCH, D_MODEL)
    return q, k, v, u, gates


def _sink_attention(q, k, v, mask, sink):
    s = jnp.einsum('nqkgd,nskd->nkgqs', q, k,
                   preferred_element_type=jnp.float32) * ATTN_SCALE
    s = jnp.where(mask, s, NEG)
    sk = sink.astype(jnp.float32).reshape(1, N_KV_HEADS, GROUP, 1, 1)
    m = jnp.maximum(jnp.max(s, axis=-1, keepdims=True), sk)
    e = jnp.exp(s - m)
    den = jnp.sum(e, axis=-1, keepdims=True) + jnp.exp(sk - m)
    p = (e / den).astype(v.dtype)
    o = jnp.einsum('nkgqs,nskd->nqkgd', p, v)
    return o.reshape(o.shape[0], o.shape[1], ATT_W)


def _prompt_attention(q, k, v, sink):
    B, S = q.shape[0], q.shape[1]
    nb = S // BLOCK
    qb = q.reshape(B * nb, BLOCK, N_KV_HEADS, GROUP, HEAD_DIM)

    def band(t):
        tb = t.reshape(B, nb, BLOCK, N_KV_HEADS, HEAD_DIM)
        prev = jnp.concatenate([jnp.zeros_like(tb[:, :1]), tb[:, :-1]], axis=1)
        return jnp.concatenate([prev, tb], axis=2).reshape(B * nb, 2 * BLOCK, N_KV_HEADS, HEAD_DIM)

    i = jnp.arange(BLOCK)[:, None]
    j = jnp.arange(2 * BLOCK)[None, :]
    diff = i + BLOCK - j
    blk = jnp.arange(nb)[:, None, None]
    mask = (diff >= 0) & (diff < WINDOW) & (blk * BLOCK - BLOCK + j >= 0)
    mask = jnp.broadcast_to(mask[None], (B, nb, BLOCK, 2 * BLOCK)).reshape(
        B * nb, 1, 1, BLOCK, 2 * BLOCK)
    o = _sink_attention(qb, band(k), band(v), mask, sink)
    return o.reshape(B, S, ATT_W)


def _sample_attention(q, k_all, v_all, sink):
    T = q.shape[1]
    qpos = PAST_LEN + jnp.arange(T)
    kpos = jnp.concatenate([PAST_LEN - WINDOW + jnp.arange(WINDOW), qpos])
    diff = qpos[:, None] - kpos[None, :]
    mask = ((diff >= 0) & (diff < WINDOW))[None, None, None]
    return _sink_attention(q, k_all, v_all, mask, sink)


def _conv_tail(ubuf, conv_w, conv_b, cln_g, cln_b, w_conv_o):
    c = lax.conv_general_dilated(ubuf, conv_w[:, None, :], (1,), 'VALID',
                                 dimension_numbers=('NWC', 'WIO', 'NWC'),
                                 feature_group_count=D_CONV) + conv_b
    c = jax.nn.silu(_layernorm(c, cln_g, cln_b))
    return c @ w_conv_o


def _finish(x, y_att, y_conv, gates, w_attn_o, w_out, norm2_g, w_up, w_down):
    ya = y_att @ w_attn_o
    mix = (gates[..., 0, :] * ya + gates[..., 1, :] * y_conv) @ w_out
    h = x + mix
    z = _rmsnorm(h, norm2_g) @ w_up
    return h + jnp.square(jax.nn.relu(z)) @ w_down


def setup_inputs(seed: int = 0) -> dict:
    key = jax.random.key(seed)
    ks = jax.random.split(key, 24)
    f32 = jnp.float32

    def nrm(k, shape, scale):
        return jax.random.normal(k, shape, f32) * scale

    return {
        "x_prompt": nrm(ks[0], (BATCH, SEQ, D_MODEL), 1.0),
        "x_sample": nrm(ks[1], (DEC_BATCH, DEC_SEQ, D_MODEL), 1.0),
        "cache_k": nrm(ks[2], (DEPTH, DEC_BATCH, WINDOW, N_KV_HEADS, HEAD_DIM), 1.0),
        "cache_v": nrm(ks[3], (DEPTH, DEC_BATCH, WINDOW, N_KV_HEADS, HEAD_DIM), 1.0),
        "state_conv": nrm(ks[4], (DEPTH, DEC_BATCH, CONV_W - 1, D_CONV), 0.5),
        "norm1_g": 1.0 + nrm(ks[5], (DEPTH, D_MODEL), 0.02),
        "w_in": nrm(ks[6], (DEPTH, D_MODEL, IN_W), D_MODEL ** -0.5),
        "b_gate": nrm(ks[7], (DEPTH, N_BRANCH * D_MODEL), 0.01),
        "sink": nrm(ks[8], (DEPTH, N_HEADS), 0.5),
        "w_attn_o": nrm(ks[9], (DEPTH, ATT_W, D_MODEL), ATT_W ** -0.5),
        "conv_w": nrm(ks[10], (DEPTH, CONV_W, D_CONV), CONV_W ** -0.5),
        "conv_b": nrm(ks[11], (DEPTH, D_CONV), 0.01),
        "cln_g": 1.0 + nrm(ks[12], (DEPTH, D_CONV), 0.02),
        "cln_b": nrm(ks[13], (DEPTH, D_CONV), 0.01),
        "w_conv_o": nrm(ks[14], (DEPTH, D_CONV, D_MODEL), D_CONV ** -0.5),
        "w_out": nrm(ks[15], (DEPTH, D_MODEL, D_MODEL), D_MODEL ** -0.5),
        "norm2_g": 1.0 + nrm(ks[16], (DEPTH, D_MODEL), 0.02),
        "w_up": nrm(ks[17], (DEPTH, D_MODEL, D_FF), D_MODEL ** -0.5),
        "w_down": nrm(ks[18], (DEPTH, D_FF, D_MODEL), D_FF ** -0.5),
        "norm_f_g": 1.0 + nrm(ks[19], (D_MODEL,), 0.02),
    }


def reference(x_prompt, x_sample, cache_k, cache_v, state_conv, norm1_g, w_in, b_gate, sink,
              w_attn_o, conv_w, conv_b, cln_g, cln_b, w_conv_o, w_out, norm2_g, w_up, w_down,
              norm_f_g):
    hp, hs = x_prompt, x_sample
    kp_l, vp_l, cp_l, ks_l, vs_l, cs_l = [], [], [], [], [], []
    for l in range(DEPTH):
        n = _rmsnorm(hp, norm1_g[l])
        q, k, v, u, g = _project(n, w_in[l], b_gate[l])
        ya = _prompt_attention(q, k, v, sink[l])
        ubuf = jnp.pad(u, ((0, 0), (CONV_W - 1, 0), (0, 0)))
        yc = _conv_tail(ubuf, conv_w[l], conv_b[l], cln_g[l], cln_b[l], w_conv_o[l])
        hp = _finish(hp, ya, yc, g, w_attn_o[l], w_out[l], norm2_g[l], w_up[l], w_down[l])
        kp_l.append(k[:, -WINDOW:])
        vp_l.append(v[:, -WINDOW:])
        cp_l.append(u[:, -(CONV_W - 1):])

        n = _rmsnorm(hs, norm1_g[l])
        q, k, v, u, g = _project(n, w_in[l], b_gate[l])
        k_all = jnp.concatenate([cache_k[l].astype(k.dtype), k], axis=1)
        v_all = jnp.concatenate([cache_v[l].astype(v.dtype), v], axis=1)
        ya = _sample_attention(q, k_all, v_all, sink[l])
        ubuf = jnp.concatenate([state_conv[l].astype(u.dtype), u], axis=1)
        yc = _conv_tail(ubuf, conv_w[l], conv_b[l], cln_g[l], cln_b[l], w_conv_o[l])
        hs = _finish(hs, ya, yc, g, w_attn_o[l], w_out[l], norm2_g[l], w_up[l], w_down[l])
        ks_l.append(k_all[:, -WINDOW:])
        vs_l.append(v_all[:, -WINDOW:])
        cs_l.append(ubuf[:, -(CONV_W - 1):])

    y_prompt = _rmsnorm(hp, norm_f_g)
    y_sample = _rmsnorm(hs, norm_f_g)
    return (y_prompt, y_sample, jnp.stack(kp_l), jnp.stack(vp_l), jnp.stack(cp_l),
            jnp.stack(ks_l), jnp.stack(vs_l), jnp.stack(cs_l))
```

```python
import functools

import jax
import jax.numpy as jnp
from jax import lax
from jax.experimental import pallas as pl
from jax.experimental.pallas import tpu as pltpu

D_MODEL = 2048
HEAD_DIM = 64
N_HEADS = 16
N_KV_HEADS = 4
GROUP = N_HEADS // N_KV_HEADS
ATT_W = N_HEADS * HEAD_DIM
KV_W = N_KV_HEADS * HEAD_DIM
WINDOW = 128
D_CONV = D_MODEL // 2
CONV_W = 31
N_BRANCH = 2
D_FF = 4 * D_MODEL
EPS = 1e-6
NEG = -1e30
ATTN_SCALE = HEAD_DIM ** -0.5

F32 = jnp.float32
BF16 = jnp.bfloat16

V7X_VMEM_LIMIT_BYTES = 58 * 1024 * 1024


def _dot(a, b):
    return jnp.dot(a, b, preferred_element_type=F32)


def _dot_nt(a, b):
    return lax.dot_general(a, b, (((1,), (1,)), ((), ())), preferred_element_type=F32)


def _sigmoid(x):
    return 1.0 / (1.0 + jnp.exp(-x))


def _rms(x, g):
    return x * lax.rsqrt(jnp.mean(x * x, axis=-1, keepdims=True) + EPS) * g


def _params(sem):
    return pltpu.CompilerParams(dimension_semantics=sem, vmem_limit_bytes=V7X_VMEM_LIMIT_BYTES)


def _resident(shape):
    return pl.BlockSpec(shape, lambda *_: (0,) * len(shape), pipeline_mode=pl.Buffered(1))


GATE_BLOCK = 1024
N_GATE_BLOCKS = N_BRANCH * D_MODEL // GATE_BLOCK
PROJ_STEPS = 3 + N_GATE_BLOCKS


def _proj_kernel(x_ref, g1_ref, wq_ref, wkv_ref, wua_ref, wub_ref, wg_ref, bg_ref,
                 q_ref, kv_ref, u_ref, gate_ref, xn_ref):
    j = pl.program_id(1)

    @pl.when(j == 0)
    def _():
        xn_ref[...] = _rms(x_ref[...], g1_ref[...]).astype(BF16)
        q_ref[...] = _dot(xn_ref[...], wq_ref[...]).astype(BF16)

    @pl.when(j == 1)
    def _():
        kv_ref[...] = _dot(xn_ref[...], wkv_ref[...])

    @pl.when(j == 2)
    def _():
        ua = _dot(xn_ref[...], wua_ref[...])
        ub = _dot(xn_ref[...], wub_ref[...])
        u_ref[...] = ua * _sigmoid(ub)

    @pl.when(j >= 3)
    def _():
        gl = _dot(xn_ref[...], wg_ref[...]) + bg_ref[...]
        gate_ref[...] = _sigmoid(gl).astype(BF16)


def _proj(x, g1, wq, wkv, wua, wub, wg, bg, tm):
    t = x.shape[0]
    gate_idx = lambda i, j: (0, jnp.maximum(j - 3, 0))
    return pl.pallas_call(
        _proj_kernel,
        grid=(t // tm, PROJ_STEPS),
        in_specs=[
            pl.BlockSpec((tm, D_MODEL), lambda i, j: (i, 0)),
            _resident((1, D_MODEL)),
            _resident((D_MODEL, ATT_W)),
            _resident((D_MODEL, 2 * KV_W)),
            _resident((D_MODEL, D_CONV)),
            _resident((D_MODEL, D_CONV)),
            pl.BlockSpec((D_MODEL, GATE_BLOCK), gate_idx),
            pl.BlockSpec((1, GATE_BLOCK), gate_idx),
        ],
        out_specs=[
            pl.BlockSpec((tm, ATT_W), lambda i, j: (i, 0)),
            pl.BlockSpec((tm, 2 * KV_W), lambda i, j: (i, 0)),
            pl.BlockSpec((tm, D_CONV), lambda i, j: (i, 0)),
            pl.BlockSpec((tm, GATE_BLOCK), lambda i, j: (i, jnp.maximum(j - 3, 0))),
        ],
        out_shape=[
            jax.ShapeDtypeStruct((t, ATT_W), BF16),
            jax.ShapeDtypeStruct((t, 2 * KV_W), F32),
            jax.ShapeDtypeStruct((t, D_CONV), F32),
            jax.ShapeDtypeStruct((t, N_BRANCH * D_MODEL), BF16),
        ],
        scratch_shapes=[pltpu.VMEM((tm, D_MODEL), BF16)],
        compiler_params=_params(("arbitrary", "arbitrary")),
        name="proj",
    )(x, g1, wq, wkv, wua, wub, wg, bg)


def _sink_softmax(s, valid, sink):
    s = jnp.where(valid, s, NEG)
    m = jnp.maximum(jnp.max(s, axis=-1, keepdims=True), sink)
    e = jnp.exp(s - m)
    den = jnp.sum(e, axis=-1, keepdims=True) + jnp.exp(sink - m)
    return e / den


ATT_Q = 512
ATT_SUB = ATT_Q // WINDOW


def _prompt_attn_kernel(sink_ref, q_ref, kvc_ref, kvp_ref, o_ref, k_buf, v_buf):
    first_tile = pl.program_id(1) == 0
    k_buf[0:WINDOW, :] = kvp_ref[:, 0:KV_W].astype(BF16)
    v_buf[0:WINDOW, :] = kvp_ref[:, KV_W:2 * KV_W].astype(BF16)
    k_buf[WINDOW:WINDOW + ATT_Q, :] = kvc_ref[:, 0:KV_W].astype(BF16)
    v_buf[WINDOW:WINDOW + ATT_Q, :] = kvc_ref[:, KV_W:2 * KV_W].astype(BF16)

    qi = lax.broadcasted_iota(jnp.int32, (WINDOW, 2 * WINDOW), 0)
    kj = lax.broadcasted_iota(jnp.int32, (WINDOW, 2 * WINDOW), 1)
    band = (kj > qi) & (kj <= qi + WINDOW)

    def sub_block(sb, carry):
        row0 = pl.multiple_of(sb * WINDOW, WINDOW)
        lo = jnp.where(first_tile & (sb == 0), WINDOW, 0)
        valid = band & (kj >= lo)
        outs = []
        for h in range(N_KV_HEADS):
            k_h = k_buf[pl.ds(row0, 2 * WINDOW), h * HEAD_DIM:(h + 1) * HEAD_DIM]
            v_h = v_buf[pl.ds(row0, 2 * WINDOW), h * HEAD_DIM:(h + 1) * HEAD_DIM]
            for g in range(GROUP):
                hd = h * GROUP + g
                q_h = q_ref[pl.ds(row0, WINDOW), hd * HEAD_DIM:(hd + 1) * HEAD_DIM]
                s = _dot_nt(q_h, k_h) * ATTN_SCALE
                p = _sink_softmax(s, valid, sink_ref[hd])
                outs.append(_dot(p.astype(BF16), v_h))
        o_ref[pl.ds(row0, WINDOW), :] = jnp.concatenate(outs, axis=-1).astype(BF16)
        return carry

    lax.fori_loop(0, ATT_SUB, sub_block, 0)


def _prompt_attn(sink, q, kv, batch, seq):
    tiles = seq // ATT_Q
    prev_idx = lambda b, i: (jnp.maximum((b * tiles + i) * ATT_SUB - 1, 0), 0)
    return pl.pallas_call(
        _prompt_attn_kernel,
        grid=(batch, tiles),
        in_specs=[
            pl.BlockSpec(memory_space=pltpu.SMEM),
            pl.BlockSpec((ATT_Q, ATT_W), lambda b, i: (b * tiles + i, 0)),
            pl.BlockSpec((ATT_Q, 2 * KV_W), lambda b, i: (b * tiles + i, 0)),
            pl.BlockSpec((WINDOW, 2 * KV_W), prev_idx),
        ],
        out_specs=pl.BlockSpec((ATT_Q, ATT_W), lambda b, i: (b * tiles + i, 0)),
        out_shape=jax.ShapeDtypeStruct((batch * seq, ATT_W), BF16),
        scratch_shapes=[pltpu.VMEM((WINDOW + ATT_Q, KV_W), BF16),
                        pltpu.VMEM((WINDOW + ATT_Q, KV_W), BF16)],
        compiler_params=_params(("arbitrary", "arbitrary")),
        name="prompt_attn",
    )(sink, q, kv, kv)


DEC_G = 8
KEY_PAD = 8


def _sample_attn_kernel(sink_ref, q_ref, kv_ref, ck_ref, cv_ref, o_ref, nk_ref, nv_ref,
                        k_buf, v_buf, *, dec_seq):
    rows = GROUP * dec_seq
    keys = WINDOW + KEY_PAD
    r = lax.broadcasted_iota(jnp.int32, (rows, keys), 0)
    c = lax.broadcasted_iota(jnp.int32, (rows, keys), 1)
    t = r % dec_seq
    valid = ((c < WINDOW) & (c > t)) | ((c >= WINDOW) & (c - WINDOW <= t))
    grp = lax.broadcasted_iota(jnp.int32, (rows, 1), 0) // dec_seq

    for b in range(DEC_G):
        k_new = kv_ref[b, :, 0:KV_W]
        v_new = kv_ref[b, :, KV_W:2 * KV_W]
        nk_ref[b, 0:WINDOW - dec_seq, :] = ck_ref[b, dec_seq:WINDOW, :]
        nv_ref[b, 0:WINDOW - dec_seq, :] = cv_ref[b, dec_seq:WINDOW, :]
        nk_ref[b, WINDOW - dec_seq:WINDOW, :] = k_new
        nv_ref[b, WINDOW - dec_seq:WINDOW, :] = v_new
        k_buf[0:WINDOW, :] = ck_ref[b].astype(BF16)
        v_buf[0:WINDOW, :] = cv_ref[b].astype(BF16)
        pad = jnp.zeros((KEY_PAD - dec_seq, KV_W), F32)
        k_buf[WINDOW:keys, :] = jnp.concatenate([k_new, pad], axis=0).astype(BF16)
        v_buf[WINDOW:keys, :] = jnp.concatenate([v_new, pad], axis=0).astype(BF16)
        for h in range(N_KV_HEADS):
            k_h = k_buf[:, h * HEAD_DIM:(h + 1) * HEAD_DIM]
            v_h = v_buf[:, h * HEAD_DIM:(h + 1) * HEAD_DIM]
            q_h = jnp.concatenate(
                [q_ref[b, :, (h * GROUP + g) * HEAD_DIM:(h * GROUP + g + 1) * HEAD_DIM]
                 for g in range(GROUP)], axis=0).astype(BF16)
            sink_col = jnp.zeros((rows, 1), F32)
            for g in range(GROUP):
                sink_col = jnp.where(grp == g, sink_ref[h * GROUP + g], sink_col)
            s = _dot_nt(q_h, k_h) * ATTN_SCALE
            p = _sink_softmax(s, valid, sink_col)
            o = _dot(p.astype(BF16), v_h)
            for g in range(GROUP):
                hd = h * GROUP + g
                o_ref[b, :, hd * HEAD_DIM:(hd + 1) * HEAD_DIM] = o[g * dec_seq:(g + 1) * dec_seq, :]


def _sample_attn(sink, q, kv, ck, cv):
    nb, dec_seq, _ = q.shape
    blk3 = lambda d1, d2: pl.BlockSpec((DEC_G, d1, d2), lambda i: (i, 0, 0))
    return pl.pallas_call(
        functools.partial(_sample_attn_kernel, dec_seq=dec_seq),
        grid=(nb // DEC_G,),
        in_specs=[
            pl.BlockSpec(memory_space=pltpu.SMEM),
            blk3(dec_seq, ATT_W), blk3(dec_seq, 2 * KV_W), blk3(WINDOW, KV_W), blk3(WINDOW, KV_W),
        ],
        out_specs=[blk3(dec_seq, ATT_W), blk3(WINDOW, KV_W), blk3(WINDOW, KV_W)],
        out_shape=[
            jax.ShapeDtypeStruct((nb, dec_seq, ATT_W), F32),
            jax.ShapeDtypeStruct((nb, WINDOW, KV_W), F32),
            jax.ShapeDtypeStruct((nb, WINDOW, KV_W), F32),
        ],
        scratch_shapes=[pltpu.VMEM((WINDOW + KEY_PAD, KV_W), BF16),
                        pltpu.VMEM((WINDOW + KEY_PAD, KV_W), BF16)],
        compiler_params=_params(("arbitrary",)),
        name="sample_attn",
    )(sink, q, kv, ck, cv)


def _ln_swish(c, g, b):
    mu = jnp.mean(c, axis=-1, keepdims=True)
    xc = c - mu
    y = xc * lax.rsqrt(jnp.mean(xc * xc, axis=-1, keepdims=True) + EPS) * g + b
    return y * _sigmoid(y)


CONV_T = 256
CONV_HALO = 32
CONV_ROWS = 64
LANES = 128


def _prompt_conv_kernel(uc_ref, up_ref, w_ref, cb_ref, g_ref, b_ref, o_ref, buf, acc):
    first_tile = pl.program_id(1) == 0
    n_lane = D_CONV // LANES
    for c in range(n_lane):
        lanes = slice(c * LANES, (c + 1) * LANES)
        halo = up_ref[:, lanes]
        buf[c, 0:CONV_HALO, :] = jnp.where(first_tile, 0.0, halo)
        buf[c, CONV_HALO:CONV_HALO + CONV_T, :] = uc_ref[:, lanes]
    off = CONV_HALO - (CONV_W - 1)
    for c in range(n_lane):
        lanes = slice(c * LANES, (c + 1) * LANES)
        for rc in range(CONV_T // CONV_ROWS):
            r0 = rc * CONV_ROWS
            a = jnp.zeros((CONV_ROWS, LANES), F32)
            for j in range(CONV_W):
                a = a + w_ref[j:j + 1, lanes] * buf[c, r0 + off + j:r0 + off + j + CONV_ROWS, :]
            acc[r0:r0 + CONV_ROWS, lanes] = a
    o_ref[...] = _ln_swish(acc[...] + cb_ref[...], g_ref[...], b_ref[...]).astype(BF16)


def _prompt_conv(u, w, cb, g, b, batch, seq):
    tiles = seq // CONV_T
    ratio = CONV_T // CONV_HALO
    prev_idx = lambda bb, i: (jnp.maximum((bb * tiles + i) * ratio - 1, 0), 0)
    return pl.pallas_call(
        _prompt_conv_kernel,
        grid=(batch, tiles),
        in_specs=[
            pl.BlockSpec((CONV_T, D_CONV), lambda bb, i: (bb * tiles + i, 0)),
            pl.BlockSpec((CONV_HALO, D_CONV), prev_idx),
            _resident((CONV_W, D_CONV)),
            _resident((1, D_CONV)), _resident((1, D_CONV)), _resident((1, D_CONV)),
        ],
        out_specs=pl.BlockSpec((CONV_T, D_CONV), lambda bb, i: (bb * tiles + i, 0)),
        out_shape=jax.ShapeDtypeStruct((batch * seq, D_CONV), BF16),
        scratch_shapes=[pltpu.VMEM((D_CONV // LANES, CONV_HALO + CONV_T, LANES), F32),
                        pltpu.VMEM((CONV_T, D_CONV), F32)],
        compiler_params=_params(("arbitrary", "arbitrary")),
        name="prompt_conv",
    )(u, u, w, cb, g, b)


def _sample_conv_kernel(st_ref, u_ref, w_ref, cb_ref, g_ref, b_ref, o_ref, ns_ref, buf, acc,
                        *, dec_seq):
    hist = CONV_W - 1
    buf[:, 0:hist, :] = st_ref[...]
    buf[:, hist:hist + dec_seq, :] = u_ref[...]
    ns_ref[...] = buf[:, dec_seq:dec_seq + hist, :]
    for b in range(DEC_G):
        a = jnp.zeros((dec_seq, D_CONV), F32)
        for j in range(CONV_W):
            a = a + w_ref[j:j + 1, :] * buf[b, j:j + dec_seq, :]
        acc[b] = a
    o_ref[...] = _ln_swish(acc[...] + cb_ref[...], g_ref[...], b_ref[...])


def _sample_conv(state, u, w, cb, g, b):
    nb, dec_seq, _ = u.shape
    hist = CONV_W - 1
    blk3 = lambda d1: pl.BlockSpec((DEC_G, d1, D_CONV), lambda i: (i, 0, 0))
    return pl.pallas_call(
        functools.partial(_sample_conv_kernel, dec_seq=dec_seq),
        grid=(nb // DEC_G,),
        in_specs=[blk3(hist), blk3(dec_seq), _resident((CONV_W, D_CONV)),
                  _resident((1, D_CONV)), _resident((1, D_CONV)), _resident((1, D_CONV))],
        out_specs=[blk3(dec_seq), blk3(hist)],
        out_shape=[jax.ShapeDtypeStruct((nb, dec_seq, D_CONV), F32),
                   jax.ShapeDtypeStruct((nb, hist, D_CONV), F32)],
        scratch_shapes=[pltpu.VMEM((DEC_G, hist + dec_seq, D_CONV), F32),
                        pltpu.VMEM((DEC_G, dec_seq, D_CONV), F32)],
        compiler_params=_params(("arbitrary",)),
        name="sample_conv",
    )(state, u, w, cb, g, b)


MIX_T = 256


def _mix_kernel(att_ref, ycp_ref, gate_ref, x_ref, wa_ref, wc_ref, wo_ref, g2_ref, h_ref, hn_ref):
    ya = _dot(att_ref[...], wa_ref[...])
    yc = _dot(ycp_ref[...], wc_ref[...])
    mixed = (gate_ref[:, 0:D_MODEL].astype(F32) * ya
             + gate_ref[:, D_MODEL:2 * D_MODEL].astype(F32) * yc)
    h = x_ref[...] + _dot(mixed.astype(BF16), wo_ref[...])
    h_ref[...] = h
    hn_ref[...] = _rms(h, g2_ref[...]).astype(BF16)


def _mix(att, ycp, gates, x, wa, wc, wo, g2):
    t = x.shape[0]
    row = lambda w: pl.BlockSpec((MIX_T, w), lambda i: (i, 0))
    return pl.pallas_call(
        _mix_kernel,
        grid=(t // MIX_T,),
        in_specs=[row(ATT_W), row(D_CONV), row(N_BRANCH * D_MODEL), row(D_MODEL),
                  _resident((ATT_W, D_MODEL)), _resident((D_CONV, D_MODEL)),
                  _resident((D_MODEL, D_MODEL)), _resident((1, D_MODEL))],
        out_specs=[row(D_MODEL), row(D_MODEL)],
        out_shape=[jax.ShapeDtypeStruct((t, D_MODEL), F32),
                   jax.ShapeDtypeStruct((t, D_MODEL), BF16)],
        compiler_params=_params(("arbitrary",)),
        name="mix",
    )(att, ycp, gates, x, wa, wc, wo, g2)


MLP_T = 512
MLP_F = 512


def _mlp_kernel(hn_ref, h_ref, wu_ref, wd_ref, gf_ref, y_ref, *, final_norm):
    f = pl.program_id(1)

    @pl.when(f == 0)
    def _():
        y_ref[...] = h_ref[...]

    z = jnp.maximum(_dot(hn_ref[...], wu_ref[...]), 0.0)
    y_ref[...] += _dot((z * z).astype(BF16), wd_ref[...])

    if final_norm:
        @pl.when(f == pl.num_programs(1) - 1)
        def _():
            y_ref[...] = _rms(y_ref[...], gf_ref[...])


def _mlp(hn, h, wu, wd, gf, final_norm):
    t = h.shape[0]
    return pl.pallas_call(
        functools.partial(_mlp_kernel, final_norm=final_norm),
        grid=(t // MLP_T, D_FF // MLP_F),
        in_specs=[
            pl.BlockSpec((MLP_T, D_MODEL), lambda i, f: (i, 0)),
            pl.BlockSpec((MLP_T, D_MODEL), lambda i, f: (i, 0)),
            pl.BlockSpec((D_MODEL, MLP_F), lambda i, f: (0, f)),
            pl.BlockSpec((MLP_F, D_MODEL), lambda i, f: (f, 0)),
            _resident((1, D_MODEL)),
        ],
        out_specs=pl.BlockSpec((MLP_T, D_MODEL), lambda i, f: (i, 0)),
        out_shape=jax.ShapeDtypeStruct((t, D_MODEL), F32),
        compiler_params=_params(("arbitrary", "arbitrary")),
        name="mlp",
    )(hn, h, wu, wd, gf)


def _split_w_in(w_in, b_gate):
    o_k = ATT_W
    o_u = ATT_W + 2 * KV_W
    o_g = o_u + 2 * D_CONV
    wb = w_in.astype(BF16)
    return (wb[:, :o_k], wb[:, o_k:o_u], wb[:, o_u:o_u + D_CONV], wb[:, o_u + D_CONV:o_g],
            wb[:, o_g:], b_gate.reshape(1, -1))


def kernel(x_prompt, x_sample, cache_k, cache_v, state_conv, norm1_g, w_in, b_gate, sink,
           w_attn_o, conv_w, conv_b, cln_g, cln_b, w_conv_o, w_out, norm2_g, w_up, w_down,
           norm_f_g):
    depth = w_in.shape[0]
    batch, seq, _ = x_prompt.shape
    nb, dec_seq, _ = x_sample.shape
    hist = CONV_W - 1
    hp = x_prompt.reshape(batch * seq, D_MODEL)
    hs = x_sample.reshape(nb * dec_seq, D_MODEL)
    row = lambda v: v.reshape(1, -1)
    gf = row(norm_f_g)
    outs = [[] for _ in range(6)]
    for l in range(depth):
        last = l == depth - 1
        wq, wkv, wua, wub, wg, bg = _split_w_in(w_in[l], b_gate[l])
        wa, wc, wo = w_attn_o[l].astype(BF16), w_conv_o[l].astype(BF16), w_out[l].astype(BF16)
        wu, wd = w_up[l].astype(BF16), w_down[l].astype(BF16)
        g1, g2 = row(norm1_g[l]), row(norm2_g[l])
        cb, lg, lb = row(conv_b[l]), row(cln_g[l]), row(cln_b[l])

        q, kv, u, gates = _proj(hp, g1, wq, wkv, wua, wub, wg, bg, tm=512)
        att = _prompt_attn(sink[l], q, kv, batch, seq)
        ycp = _prompt_conv(u, conv_w[l], cb, lg, lb, batch, seq)
        h, hn = _mix(att, ycp, gates, hp, wa, wc, wo, g2)
        hp = _mlp(hn, h, wu, wd, gf, final_norm=last)
        kv3 = kv.reshape(batch, seq, 2 * KV_W)
        outs[0].append(kv3[:, seq - WINDOW:, :KV_W].reshape(batch, WINDOW, N_KV_HEADS, HEAD_DIM))
        outs[1].append(kv3[:, seq - WINDOW:, KV_W:].reshape(batch, WINDOW, N_KV_HEADS, HEAD_DIM))
        outs[2].append(u.reshape(batch, seq, D_CONV)[:, seq - hist:, :])

        q, kv, u, gates = _proj(hs, g1, wq, wkv, wua, wub, wg, bg, tm=nb * dec_seq)
        att, nk, nv = _sample_attn(
            sink[l], q.astype(F32).reshape(nb, dec_seq, ATT_W), kv.reshape(nb, dec_seq, 2 * KV_W),
            cache_k[l].reshape(nb, WINDOW, KV_W), cache_v[l].reshape(nb, WINDOW, KV_W))
        ycp, ns = _sample_conv(state_conv[l], u.reshape(nb, dec_seq, D_CONV), conv_w[l], cb, lg, lb)
        h, hn = _mix(att.reshape(nb * dec_seq, ATT_W).astype(BF16),
                     ycp.reshape(nb * dec_seq, D_CONV).astype(BF16), gates, hs, wa, wc, wo, g2)
        hs = _mlp(hn, h, wu, wd, gf, final_norm=last)
        outs[3].append(nk.reshape(nb, WINDOW, N_KV_HEADS, HEAD_DIM))
        outs[4].append(nv.reshape(nb, WINDOW, N_KV_HEADS, HEAD_DIM))
        outs[5].append(ns)

    return (hp.reshape(batch, seq, D_MODEL), hs.reshape(nb, dec_seq, D_MODEL),
            *[jnp.stack(o) for o in outs])
```

```python
import functools

import jax
import jax.numpy as jnp
from jax import lax
from jax.experimental import pallas as pl
from jax.experimental.pallas import tpu as pltpu

D_MODEL = 2048
HEAD_DIM = 64
N_HEADS = 16
N_KV_HEADS = 4
GROUP = N_HEADS // N_KV_HEADS
ATT_W = N_HEADS * HEAD_DIM
KV_W = N_KV_HEADS * HEAD_DIM
WINDOW = 128
D_CONV = D_MODEL // 2
CONV_W = 31
N_BRANCH = 2
D_FF = 4 * D_MODEL
EPS = 1e-6
NEG = -1e30
ATTN_SCALE = HEAD_DIM ** -0.5

F32 = jnp.float32
BF16 = jnp.bfloat16

V7X_VMEM_LIMIT_BYTES = 58 * 1024 * 1024


def _dot(a, b):
    return jnp.dot(a, b, preferred_element_type=F32)


def _dot_nt(a, b):
    return lax.dot_general(a, b, (((1,), (1,)), ((), ())), preferred_element_type=F32)


def _sigmoid(x):
    return 1.0 / (1.0 + jnp.exp(-x))


def _rms(x, g):
    return x * lax.rsqrt(jnp.mean(x * x, axis=-1, keepdims=True) + EPS) * g


def _params(sem):
    return pltpu.CompilerParams(dimension_semantics=sem, vmem_limit_bytes=V7X_VMEM_LIMIT_BYTES)


def _resident(shape):
    return pl.BlockSpec(shape, lambda *_: (0,) * len(shape), pipeline_mode=pl.Buffered(1))


GATE_BLOCK = 1024
N_GATE_BLOCKS = N_BRANCH * D_MODEL // GATE_BLOCK
PROJ_STEPS = 3 + N_GATE_BLOCKS


def _proj_kernel(x_ref, g1_ref, wq_ref, wkv_ref, wua_ref, wub_ref, wg_ref, bg_ref,
                 q_ref, kv_ref, u_ref, gate_ref, xn_ref):
    j = pl.program_id(1)

    @pl.when(j == 0)
    def _():
        xn_ref[...] = _rms(x_ref[...], g1_ref[...]).astype(BF16)
        q_ref[...] = _dot(xn_ref[...], wq_ref[...]).astype(BF16)

    @pl.when(j == 1)
    def _():
        kv_ref[...] = _dot(xn_ref[...], wkv_ref[...])

    @pl.when(j == 2)
    def _():
        ua = _dot(xn_ref[...], wua_ref[...])
        ub = _dot(xn_ref[...], wub_ref[...])
        u_ref[...] = ua * _sigmoid(ub)

    @pl.when(j >= 3)
    def _():
        gl = _dot(xn_ref[...], wg_ref[...]) + bg_ref[...]
        gate_ref[...] = _sigmoid(gl).astype(BF16)


def _proj(x, g1, wq, wkv, wua, wub, wg, bg, tm):
    t = x.shape[0]
    gate_idx = lambda i, j: (0, jnp.maximum(j - 3, 0))
    return pl.pallas_call(
        _proj_kernel,
        grid=(t // tm, PROJ_STEPS),
        in_specs=[
            pl.BlockSpec((tm, D_MODEL), lambda i, j: (i, 0)),
            _resident((1, D_MODEL)),
            _resident((D_MODEL, ATT_W)),
            _resident((D_MODEL, 2 * KV_W)),
            _resident((D_MODEL, D_CONV)),
            _resident((D_MODEL, D_CONV)),
            pl.BlockSpec((None, D_MODEL, GATE_BLOCK), lambda i, j: (jnp.maximum(j - 3, 0), 0, 0)),
            pl.BlockSpec((1, GATE_BLOCK), gate_idx),
        ],
        out_specs=[
            pl.BlockSpec((tm, ATT_W), lambda i, j: (i, 0)),
            pl.BlockSpec((tm, 2 * KV_W), lambda i, j: (i, 0)),
            pl.BlockSpec((tm, D_CONV), lambda i, j: (i, 0)),
            pl.BlockSpec((tm, GATE_BLOCK), lambda i, j: (i, jnp.maximum(j - 3, 0))),
        ],
        out_shape=[
            jax.ShapeDtypeStruct((t, ATT_W), BF16),
            jax.ShapeDtypeStruct((t, 2 * KV_W), F32),
            jax.ShapeDtypeStruct((t, D_CONV), F32),
            jax.ShapeDtypeStruct((t, N_BRANCH * D_MODEL), BF16),
        ],
        scratch_shapes=[pltpu.VMEM((tm, D_MODEL), BF16)],
        compiler_params=_params(("arbitrary", "arbitrary")),
        name="proj",
    )(x, g1, wq, wkv, wua, wub, wg, bg)


def _sink_softmax(s, valid, sink):
    s = jnp.where(valid, s, NEG)
    m = jnp.maximum(jnp.max(s, axis=-1, keepdims=True), sink)
    e = jnp.exp(s - m)
    den = jnp.sum(e, axis=-1, keepdims=True) + jnp.exp(sink - m)
    return e / den


ATT_Q = 512
ATT_SUB = ATT_Q // WINDOW


def _prompt_attn_kernel(sink_ref, q_ref, kvc_ref, kvp_ref, o_ref, k_buf, v_buf):
    first_tile = pl.program_id(1) == 0
    k_buf[0:WINDOW, :] = kvp_ref[:, 0:KV_W].astype(BF16)
    v_buf[0:WINDOW, :] = kvp_ref[:, KV_W:2 * KV_W].astype(BF16)
    k_buf[WINDOW:WINDOW + ATT_Q, :] = kvc_ref[:, 0:KV_W].astype(BF16)
    v_buf[WINDOW:WINDOW + ATT_Q, :] = kvc_ref[:, KV_W:2 * KV_W].astype(BF16)

    qi = lax.broadcasted_iota(jnp.int32, (WINDOW, 2 * WINDOW), 0)
    kj = lax.broadcasted_iota(jnp.int32, (WINDOW, 2 * WINDOW), 1)
    band = (kj > qi) & (kj <= qi + WINDOW)

    def sub_block(sb, carry):
        row0 = pl.multiple_of(sb * WINDOW, WINDOW)
        lo = jnp.where(first_tile & (sb == 0), WINDOW, 0)
        valid = band & (kj >= lo)
        outs = []
        for h in range(N_KV_HEADS):
            k_h = k_buf[pl.ds(row0, 2 * WINDOW), h * HEAD_DIM:(h + 1) * HEAD_DIM]
            v_h = v_buf[pl.ds(row0, 2 * WINDOW), h * HEAD_DIM:(h + 1) * HEAD_DIM]
            for g in range(GROUP):
                hd = h * GROUP + g
                q_h = q_ref[pl.ds(row0, WINDOW), hd * HEAD_DIM:(hd + 1) * HEAD_DIM]
                s = _dot_nt(q_h, k_h) * ATTN_SCALE
                p = _sink_softmax(s, valid, sink_ref[hd])
                outs.append(_dot(p.astype(BF16), v_h))
        o_ref[pl.ds(row0, WINDOW), :] = jnp.concatenate(outs, axis=-1).astype(BF16)
        return carry

    lax.fori_loop(0, ATT_SUB, sub_block, 0)


def _prompt_attn(sink, q, kv, batch, seq):
    tiles = seq // ATT_Q
    prev_idx = lambda b, i: (jnp.maximum((b * tiles + i) * ATT_SUB - 1, 0), 0)
    return pl.pallas_call(
        _prompt_attn_kernel,
        grid=(batch, tiles),
        in_specs=[
            pl.BlockSpec(memory_space=pltpu.SMEM),
            pl.BlockSpec((ATT_Q, ATT_W), lambda b, i: (b * tiles + i, 0)),
            pl.BlockSpec((ATT_Q, 2 * KV_W), lambda b, i: (b * tiles + i, 0)),
            pl.BlockSpec((WINDOW, 2 * KV_W), prev_idx),
        ],
        out_specs=pl.BlockSpec((ATT_Q, ATT_W), lambda b, i: (b * tiles + i, 0)),
        out_shape=jax.ShapeDtypeStruct((batch * seq, ATT_W), BF16),
        scratch_shapes=[pltpu.VMEM((WINDOW + ATT_Q, KV_W), BF16),
                        pltpu.VMEM((WINDOW + ATT_Q, KV_W), BF16)],
        compiler_params=_params(("arbitrary", "arbitrary")),
        name="prompt_attn",
    )(sink, q, kv, kv)


DEC_G = 8
KEY_PAD = 8


def _sample_attn_kernel(sink_ref, q_ref, kv_ref, ck_ref, cv_ref, o_ref, nk_ref, nv_ref,
                        *, dec_seq):
    rows = GROUP * dec_seq
    keys = WINDOW + KEY_PAD
    n_chain = DEC_G * N_KV_HEADS
    pad = jnp.zeros((KEY_PAD - dec_seq, KV_W), F32)

    scores, values = [], []
    for b in range(DEC_G):
        k_new = kv_ref[b, :, 0:KV_W]
        v_new = kv_ref[b, :, KV_W:2 * KV_W]
        nk_ref[b, 0:WINDOW - dec_seq, :] = ck_ref[b, dec_seq:WINDOW, :]
        nv_ref[b, 0:WINDOW - dec_seq, :] = cv_ref[b, dec_seq:WINDOW, :]
        nk_ref[b, WINDOW - dec_seq:WINDOW, :] = k_new
        nv_ref[b, WINDOW - dec_seq:WINDOW, :] = v_new
        k_all = jnp.concatenate([ck_ref[b], k_new, pad], axis=0).astype(BF16)
        v_all = jnp.concatenate([cv_ref[b], v_new, pad], axis=0).astype(BF16)
        for h in range(N_KV_HEADS):
            scores.append(_dot_nt(q_ref[b, h], k_all[:, h * HEAD_DIM:(h + 1) * HEAD_DIM]))
            values.append(v_all[:, h * HEAD_DIM:(h + 1) * HEAD_DIM])

    s = jnp.concatenate(scores, axis=0) * ATTN_SCALE
    r = lax.broadcasted_iota(jnp.int32, s.shape, 0)
    c = lax.broadcasted_iota(jnp.int32, s.shape, 1)
    t = r % dec_seq
    valid = ((c < WINDOW) & (c > t)) | ((c >= WINDOW) & (c - WINDOW <= t))
    head = (lax.broadcasted_iota(jnp.int32, (s.shape[0], 1), 0) // dec_seq) % N_HEADS
    sink_col = jnp.zeros((s.shape[0], 1), F32)
    for hd in range(N_HEADS):
        sink_col = jnp.where(head == hd, sink_ref[hd], sink_col)
    p = _sink_softmax(s, valid, sink_col).astype(BF16)

    for i in range(n_chain):
        o_ref[i // N_KV_HEADS, i % N_KV_HEADS] = _dot(p[i * rows:(i + 1) * rows, :], values[i])


def _sample_attn(sink, q, kv, ck, cv):
    nb, dec_seq, _ = kv.shape
    rows = GROUP * dec_seq
    blk3 = lambda d1, d2: pl.BlockSpec((DEC_G, d1, d2), lambda i: (i, 0, 0))
    blk4 = pl.BlockSpec((DEC_G, N_KV_HEADS, rows, HEAD_DIM), lambda i: (i, 0, 0, 0))
    return pl.pallas_call(
        functools.partial(_sample_attn_kernel, dec_seq=dec_seq),
        grid=(nb // DEC_G,),
        in_specs=[
            pl.BlockSpec(memory_space=pltpu.SMEM),
            blk4, blk3(dec_seq, 2 * KV_W), blk3(WINDOW, KV_W), blk3(WINDOW, KV_W),
        ],
        out_specs=[blk4, blk3(WINDOW, KV_W), blk3(WINDOW, KV_W)],
        out_shape=[
            jax.ShapeDtypeStruct((nb, N_KV_HEADS, rows, HEAD_DIM), F32),
            jax.ShapeDtypeStruct((nb, WINDOW, KV_W), F32),
            jax.ShapeDtypeStruct((nb, WINDOW, KV_W), F32),
        ],
        compiler_params=_params(("arbitrary",)),
        name="sample_attn",
    )(sink, q, kv, ck, cv)


def _to_head_major(q, nb, dec_seq):
    q = q.reshape(nb, dec_seq, N_KV_HEADS, GROUP, HEAD_DIM).transpose(0, 2, 3, 1, 4)
    return q.reshape(nb, N_KV_HEADS, GROUP * dec_seq, HEAD_DIM)


def _from_head_major(o, nb, dec_seq):
    o = o.reshape(nb, N_KV_HEADS, GROUP, dec_seq, HEAD_DIM).transpose(0, 3, 1, 2, 4)
    return o.reshape(nb * dec_seq, ATT_W)


def _ln_swish(c, g, b):
    mu = jnp.mean(c, axis=-1, keepdims=True)
    xc = c - mu
    y = xc * lax.rsqrt(jnp.mean(xc * xc, axis=-1, keepdims=True) + EPS) * g + b
    return y * _sigmoid(y)


CONV_T = 256
CONV_HALO = 32
CONV_ROWS = 64
LANES = 128


def _prompt_conv_kernel(uc_ref, up_ref, w_ref, cb_ref, g_ref, b_ref, o_ref, buf, acc):
    first_tile = pl.program_id(1) == 0
    n_lane = D_CONV // LANES
    for c in range(n_lane):
        lanes = slice(c * LANES, (c + 1) * LANES)
        halo = up_ref[:, lanes]
        buf[c, 0:CONV_HALO, :] = jnp.where(first_tile, 0.0, halo)
        buf[c, CONV_HALO:CONV_HALO + CONV_T, :] = uc_ref[:, lanes]
    off = CONV_HALO - (CONV_W - 1)
    for c in range(n_lane):
        lanes = slice(c * LANES, (c + 1) * LANES)
        for rc in range(CONV_T // CONV_ROWS):
            r0 = rc * CONV_ROWS
            a = jnp.zeros((CONV_ROWS, LANES), F32)
            for j in range(CONV_W):
                a = a + w_ref[j:j + 1, lanes] * buf[c, r0 + off + j:r0 + off + j + CONV_ROWS, :]
            acc[r0:r0 + CONV_ROWS, lanes] = a
    o_ref[...] = _ln_swish(acc[...] + cb_ref[...], g_ref[...], b_ref[...]).astype(BF16)


def _prompt_conv(u, w, cb, g, b, batch, seq):
    tiles = seq // CONV_T
    ratio = CONV_T // CONV_HALO
    prev_idx = lambda bb, i: (jnp.maximum((bb * tiles + i) * ratio - 1, 0), 0)
    return pl.pallas_call(
        _prompt_conv_kernel,
        grid=(batch, tiles),
        in_specs=[
            pl.BlockSpec((CONV_T, D_CONV), lambda bb, i: (bb * tiles + i, 0)),
            pl.BlockSpec((CONV_HALO, D_CONV), prev_idx),
            _resident((CONV_W, D_CONV)),
            _resident((1, D_CONV)), _resident((1, D_CONV)), _resident((1, D_CONV)),
        ],
        out_specs=pl.BlockSpec((CONV_T, D_CONV), lambda bb, i: (bb * tiles + i, 0)),
        out_shape=jax.ShapeDtypeStruct((batch * seq, D_CONV), BF16),
        scratch_shapes=[pltpu.VMEM((D_CONV // LANES, CONV_HALO + CONV_T, LANES), F32),
                        pltpu.VMEM((CONV_T, D_CONV), F32)],
        compiler_params=_params(("arbitrary", "arbitrary")),
        name="prompt_conv",
    )(u, u, w, cb, g, b)


def _sample_conv_kernel(st_ref, u_ref, w_ref, cb_ref, g_ref, b_ref, o_ref, ns_ref, buf, acc,
                        *, dec_seq):
    hist = CONV_W - 1
    buf[:, 0:hist, :] = st_ref[...]
    buf[:, hist:hist + dec_seq, :] = u_ref[...]
    ns_ref[...] = buf[:, dec_seq:dec_seq + hist, :]
    for b in range(DEC_G):
        a = jnp.zeros((dec_seq, D_CONV), F32)
        for j in range(CONV_W):
            a = a + w_ref[j:j + 1, :] * buf[b, j:j + dec_seq, :]
        acc[b] = a
    o_ref[...] = _ln_swish(acc[...] + cb_ref[...], g_ref[...], b_ref[...])


def _sample_conv(state, u, w, cb, g, b):
    nb, dec_seq, _ = u.shape
    hist = CONV_W - 1
    blk3 = lambda d1: pl.BlockSpec((DEC_G, d1, D_CONV), lambda i: (i, 0, 0))
    return pl.pallas_call(
        functools.partial(_sample_conv_kernel, dec_seq=dec_seq),
        grid=(nb // DEC_G,),
        in_specs=[blk3(hist), blk3(dec_seq), _resident((CONV_W, D_CONV)),
                  _resident((1, D_CONV)), _resident((1, D_CONV)), _resident((1, D_CONV))],
        out_specs=[blk3(dec_seq), blk3(hist)],
        out_shape=[jax.ShapeDtypeStruct((nb, dec_seq, D_CONV), F32),
                   jax.ShapeDtypeStruct((nb, hist, D_CONV), F32)],
        scratch_shapes=[pltpu.VMEM((DEC_G, hist + dec_seq, D_CONV), F32),
                        pltpu.VMEM((DEC_G, dec_seq, D_CONV), F32)],
        compiler_params=_params(("arbitrary",)),
        name="sample_conv",
    )(state, u, w, cb, g, b)


MIX_T = 256


def _mix_kernel(att_ref, ycp_ref, gate_ref, x_ref, wa_ref, wc_ref, wo_ref, g2_ref, h_ref, hn_ref):
    ya = _dot(att_ref[...], wa_ref[...])
    yc = _dot(ycp_ref[...], wc_ref[...])
    mixed = (gate_ref[:, 0:D_MODEL].astype(F32) * ya
             + gate_ref[:, D_MODEL:2 * D_MODEL].astype(F32) * yc)
    h = x_ref[...] + _dot(mixed.astype(BF16), wo_ref[...])
    h_ref[...] = h
    hn_ref[...] = _rms(h, g2_ref[...]).astype(BF16)


def _mix(att, ycp, gates, x, wa, wc, wo, g2):
    t = x.shape[0]
    row = lambda w: pl.BlockSpec((MIX_T, w), lambda i: (i, 0))
    return pl.pallas_call(
        _mix_kernel,
        grid=(t // MIX_T,),
        in_specs=[row(ATT_W), row(D_CONV), row(N_BRANCH * D_MODEL), row(D_MODEL),
                  _resident((ATT_W, D_MODEL)), _resident((D_CONV, D_MODEL)),
                  _resident((D_MODEL, D_MODEL)), _resident((1, D_MODEL))],
        out_specs=[row(D_MODEL), row(D_MODEL)],
        out_shape=[jax.ShapeDtypeStruct((t, D_MODEL), F32),
                   jax.ShapeDtypeStruct((t, D_MODEL), BF16)],
        compiler_params=_params(("arbitrary",)),
        name="mix",
    )(att, ycp, gates, x, wa, wc, wo, g2)


MLP_T = 512
MLP_F = 512


def _mlp_kernel(hn_ref, h_ref, wu_ref, wd_ref, gf_ref, y_ref, *, final_norm):
    f = pl.program_id(1)

    @pl.when(f == 0)
    def _():
        y_ref[...] = h_ref[...]

    z = jnp.maximum(_dot(hn_ref[...], wu_ref[...]), 0.0)
    y_ref[...] += _dot((z * z).astype(BF16), wd_ref[...])

    if final_norm:
        @pl.when(f == pl.num_programs(1) - 1)
        def _():
            y_ref[...] = _rms(y_ref[...], gf_ref[...])


def _mlp(hn, h, wu, wd, gf, final_norm):
    t = h.shape[0]
    return pl.pallas_call(
        functools.partial(_mlp_kernel, final_norm=final_norm),
        grid=(t // MLP_T, D_FF // MLP_F),
        in_specs=[
            pl.BlockSpec((MLP_T, D_MODEL), lambda i, f: (i, 0)),
            pl.BlockSpec((MLP_T, D_MODEL), lambda i, f: (i, 0)),
            pl.BlockSpec((None, D_MODEL, MLP_F), lambda i, f: (f, 0, 0)),
            pl.BlockSpec((MLP_F, D_MODEL), lambda i, f: (f, 0)),
            _resident((1, D_MODEL)),
        ],
        out_specs=pl.BlockSpec((MLP_T, D_MODEL), lambda i, f: (i, 0)),
        out_shape=jax.ShapeDtypeStruct((t, D_MODEL), F32),
        compiler_params=_params(("arbitrary", "arbitrary")),
        name="mlp",
    )(hn, h, wu, wd, gf)


def _col_blocks(w, width):
    k, n = w.shape
    return w.reshape(k, n // width, width).transpose(1, 0, 2)


def _split_w_in(w_in, b_gate):
    o_k = ATT_W
    o_u = ATT_W + 2 * KV_W
    o_g = o_u + 2 * D_CONV
    wb = w_in.astype(BF16)
    return (wb[:, :o_k], wb[:, o_k:o_u], wb[:, o_u:o_u + D_CONV], wb[:, o_u + D_CONV:o_g],
            _col_blocks(wb[:, o_g:], GATE_BLOCK), b_gate.reshape(1, -1))


def kernel(x_prompt, x_sample, cache_k, cache_v, state_conv, norm1_g, w_in, b_gate, sink,
           w_attn_o, conv_w, conv_b, cln_g, cln_b, w_conv_o, w_out, norm2_g, w_up, w_down,
           norm_f_g):
    depth = w_in.shape[0]
    batch, seq, _ = x_prompt.shape
    nb, dec_seq, _ = x_sample.shape
    hist = CONV_W - 1
    hp = x_prompt.reshape(batch * seq, D_MODEL)
    hs = x_sample.reshape(nb * dec_seq, D_MODEL)
    row = lambda v: v.reshape(1, -1)
    gf = row(norm_f_g)
    outs = [[] for _ in range(6)]
    for l in range(depth):
        last = l == depth - 1
        wq, wkv, wua, wub, wg, bg = _split_w_in(w_in[l], b_gate[l])
        wa, wc, wo = w_attn_o[l].astype(BF16), w_conv_o[l].astype(BF16), w_out[l].astype(BF16)
        wu, wd = _col_blocks(w_up[l].astype(BF16), MLP_F), w_down[l].astype(BF16)
        g1, g2 = row(norm1_g[l]), row(norm2_g[l])
        cb, lg, lb = row(conv_b[l]), row(cln_g[l]), row(cln_b[l])

        q, kv, u, gates = _proj(hp, g1, wq, wkv, wua, wub, wg, bg, tm=512)
        att = _prompt_attn(sink[l], q, kv, batch, seq)
        ycp = _prompt_conv(u, conv_w[l], cb, lg, lb, batch, seq)
        h, hn = _mix(att, ycp, gates, hp, wa, wc, wo, g2)
        hp = _mlp(hn, h, wu, wd, gf, final_norm=last)
        kv3 = kv.reshape(batch, seq, 2 * KV_W)
        outs[0].append(kv3[:, seq - WINDOW:, :KV_W].reshape(batch, WINDOW, N_KV_HEADS, HEAD_DIM))
        outs[1].append(kv3[:, seq - WINDOW:, KV_W:].reshape(batch, WINDOW, N_KV_HEADS, HEAD_DIM))
        outs[2].append(u.reshape(batch, seq, D_CONV)[:, seq - hist:, :])

        q, kv, u, gates = _proj(hs, g1, wq, wkv, wua, wub, wg, bg, tm=nb * dec_seq)
        att, nk, nv = _sample_attn(
            sink[l], _to_head_major(q, nb, dec_seq), kv.reshape(nb, dec_seq, 2 * KV_W),
            cache_k[l].reshape(nb, WINDOW, KV_W), cache_v[l].reshape(nb, WINDOW, KV_W))
        ycp, ns = _sample_conv(state_conv[l], u.reshape(nb, dec_seq, D_CONV), conv_w[l], cb, lg, lb)
        h, hn = _mix(_from_head_major(att, nb, dec_seq).astype(BF16),
                     ycp.reshape(nb * dec_seq, D_CONV).astype(BF16), gates, hs, wa, wc, wo, g2)
        hs = _mlp(hn, h, wu, wd, gf, final_norm=last)
        outs[3].append(nk.reshape(nb, WINDOW, N_KV_HEADS, HEAD_DIM))
        outs[4].append(nv.reshape(nb, WINDOW, N_KV_HEADS, HEAD_DIM))
        outs[5].append(ns)

    return (hp.reshape(batch, seq, D_MODEL), hs.reshape(nb, dec_seq, D_MODEL),
            *[jnp.stack(o) for o in outs])
```

```python
import functools

import jax
import jax.numpy as jnp
from jax import lax
from jax.experimental import pallas as pl
from jax.experimental.pallas import tpu as pltpu

D_MODEL = 2048
HEAD_DIM = 64
N_HEADS = 16
N_KV_HEADS = 4
GROUP = N_HEADS // N_KV_HEADS
ATT_W = N_HEADS * HEAD_DIM
KV_W = N_KV_HEADS * HEAD_DIM
WINDOW = 128
D_CONV = D_MODEL // 2
CONV_W = 31
N_BRANCH = 2
D_FF = 4 * D_MODEL
EPS = 1e-6
NEG = -1e30
ATTN_SCALE = HEAD_DIM ** -0.5

F32 = jnp.float32
BF16 = jnp.bfloat16

V7X_VMEM_LIMIT_BYTES = 58 * 1024 * 1024


def _dot(a, b):
    return jnp.dot(a, b, preferred_element_type=F32)


def _dot_nt(a, b):
    return lax.dot_general(a, b, (((1,), (1,)), ((), ())), preferred_element_type=F32)


def _sigmoid(x):
    return 1.0 / (1.0 + jnp.exp(-x))


def _rms(x, g):
    return x * lax.rsqrt(jnp.mean(x * x, axis=-1, keepdims=True) + EPS) * g


def _params(sem):
    return pltpu.CompilerParams(dimension_semantics=sem, vmem_limit_bytes=V7X_VMEM_LIMIT_BYTES)


def _resident(shape):
    return pl.BlockSpec(shape, lambda *_: (0,) * len(shape), pipeline_mode=pl.Buffered(1))


PROJ_T = 256
CONV_HALO = 32
CONV_ROWS = 64
LANES = 128


def _ln_swish(c, g, b):
    mu = jnp.mean(c, axis=-1, keepdims=True)
    xc = c - mu
    y = xc * lax.rsqrt(jnp.mean(xc * xc, axis=-1, keepdims=True) + EPS) * g + b
    return y * _sigmoid(y)


def _proj_kernel(x_ref, g1_ref, wq_ref, wkv_ref, wua_ref, wub_ref, wg_ref, bg_ref,
                 q_ref, kv_ref, u_ref, gate_ref):
    xn = _rms(x_ref[...], g1_ref[...]).astype(BF16)
    q_ref[...] = _dot(xn, wq_ref[...]).astype(BF16)
    kv_ref[...] = _dot(xn, wkv_ref[...])
    u_ref[...] = _dot(xn, wua_ref[...]) * _sigmoid(_dot(xn, wub_ref[...]))
    gate_ref[...] = _sigmoid(_dot(xn, wg_ref[...]) + bg_ref[...]).astype(BF16)


def _proj_conv_kernel(x_ref, g1_ref, wq_ref, wkv_ref, wua_ref, wub_ref, wg_ref, bg_ref,
                      cw_ref, cb_ref, lg_ref, lb_ref,
                      q_ref, kv_ref, gate_ref, ycp_ref, tail_ref, buf, acc, *, tiles_per_seq):
    first = pl.program_id(0) % tiles_per_seq == 0

    @pl.when(first)
    def _():
        buf[:, 0:CONV_HALO, :] = jnp.zeros((D_CONV // LANES, CONV_HALO, LANES), F32)

    @pl.when(jnp.logical_not(first))
    def _():
        buf[:, 0:CONV_HALO, :] = buf[:, PROJ_T:PROJ_T + CONV_HALO, :]

    xn = _rms(x_ref[...], g1_ref[...]).astype(BF16)
    q_ref[...] = _dot(xn, wq_ref[...]).astype(BF16)
    kv_ref[...] = _dot(xn, wkv_ref[...])
    u = _dot(xn, wua_ref[...]) * _sigmoid(_dot(xn, wub_ref[...]))
    tail_ref[...] = u[PROJ_T - CONV_HALO:, :]
    for c in range(D_CONV // LANES):
        buf[c, CONV_HALO:CONV_HALO + PROJ_T, :] = u[:, c * LANES:(c + 1) * LANES]
    off = CONV_HALO - (CONV_W - 1)
    for c in range(D_CONV // LANES):
        lanes = slice(c * LANES, (c + 1) * LANES)
        for r0 in range(0, PROJ_T, CONV_ROWS):
            a = jnp.zeros((CONV_ROWS, LANES), F32)
            for j in range(CONV_W):
                a = a + cw_ref[j:j + 1, lanes] * buf[c, r0 + off + j:r0 + off + j + CONV_ROWS, :]
            acc[r0:r0 + CONV_ROWS, lanes] = a
    ycp_ref[...] = _ln_swish(acc[...] + cb_ref[...], lg_ref[...], lb_ref[...]).astype(BF16)
    gate_ref[...] = _sigmoid(_dot(xn, wg_ref[...]) + bg_ref[...]).astype(BF16)


def _proj_specs():
    return [
        pl.BlockSpec((PROJ_T, D_MODEL), lambda i: (i, 0)),
        _resident((1, D_MODEL)),
        _resident((D_MODEL, ATT_W)),
        _resident((D_MODEL, 2 * KV_W)),
        _resident((D_MODEL, D_CONV)),
        _resident((D_MODEL, D_CONV)),
        _resident((D_MODEL, N_BRANCH * D_MODEL)),
        _resident((1, N_BRANCH * D_MODEL)),
    ]


def _proj_conv(x, g1, wq, wkv, wua, wub, wg, bg, cw, cb, lg, lb, seq):
    t = x.shape[0]
    tiles = t // PROJ_T
    row = lambda w: pl.BlockSpec((PROJ_T, w), lambda i: (i, 0))
    return pl.pallas_call(
        functools.partial(_proj_conv_kernel, tiles_per_seq=seq // PROJ_T),
        grid=(tiles,),
        in_specs=_proj_specs() + [_resident((CONV_W, D_CONV)), _resident((1, D_CONV)),
                                  _resident((1, D_CONV)), _resident((1, D_CONV))],
        out_specs=[row(ATT_W), row(2 * KV_W), row(N_BRANCH * D_MODEL), row(D_CONV),
                   pl.BlockSpec((CONV_HALO, D_CONV), lambda i: (i, 0))],
        out_shape=[
            jax.ShapeDtypeStruct((t, ATT_W), BF16),
            jax.ShapeDtypeStruct((t, 2 * KV_W), F32),
            jax.ShapeDtypeStruct((t, N_BRANCH * D_MODEL), BF16),
            jax.ShapeDtypeStruct((t, D_CONV), BF16),
            jax.ShapeDtypeStruct((tiles * CONV_HALO, D_CONV), F32),
        ],
        scratch_shapes=[pltpu.VMEM((D_CONV // LANES, CONV_HALO + PROJ_T, LANES), F32),
                        pltpu.VMEM((PROJ_T, D_CONV), F32)],
        compiler_params=_params(("arbitrary",)),
        name="proj_conv",
    )(x, g1, wq, wkv, wua, wub, wg, bg, cw, cb, lg, lb)


def _proj(x, g1, wq, wkv, wua, wub, wg, bg):
    t = x.shape[0]
    row = lambda w: pl.BlockSpec((PROJ_T, w), lambda i: (i, 0))
    return pl.pallas_call(
        _proj_kernel,
        grid=(t // PROJ_T,),
        in_specs=_proj_specs(),
        out_specs=[row(ATT_W), row(2 * KV_W), row(D_CONV), row(N_BRANCH * D_MODEL)],
        out_shape=[
            jax.ShapeDtypeStruct((t, ATT_W), BF16),
            jax.ShapeDtypeStruct((t, 2 * KV_W), F32),
            jax.ShapeDtypeStruct((t, D_CONV), F32),
            jax.ShapeDtypeStruct((t, N_BRANCH * D_MODEL), BF16),
        ],
        compiler_params=_params(("arbitrary",)),
        name="proj",
    )(x, g1, wq, wkv, wua, wub, wg, bg)


def _sink_softmax(s, valid, sink):
    s = jnp.where(valid, s, NEG)
    m = jnp.maximum(jnp.max(s, axis=-1, keepdims=True), sink)
    e = jnp.exp(s - m)
    den = jnp.sum(e, axis=-1, keepdims=True) + jnp.exp(sink - m)
    return e / den


ATT_Q = 512
ATT_SUB = ATT_Q // WINDOW


def _prompt_attn_kernel(sink_ref, q_ref, kvc_ref, kvp_ref, o_ref, k_buf, v_buf):
    first_tile = pl.program_id(1) == 0
    k_buf[0:WINDOW, :] = kvp_ref[:, 0:KV_W].astype(BF16)
    v_buf[0:WINDOW, :] = kvp_ref[:, KV_W:2 * KV_W].astype(BF16)
    k_buf[WINDOW:WINDOW + ATT_Q, :] = kvc_ref[:, 0:KV_W].astype(BF16)
    v_buf[WINDOW:WINDOW + ATT_Q, :] = kvc_ref[:, KV_W:2 * KV_W].astype(BF16)

    qi = lax.broadcasted_iota(jnp.int32, (WINDOW, 2 * WINDOW), 0)
    kj = lax.broadcasted_iota(jnp.int32, (WINDOW, 2 * WINDOW), 1)
    band = (kj > qi) & (kj <= qi + WINDOW)
    head = lax.broadcasted_iota(jnp.int32, (N_HEADS, 1, 1), 0)
    sink = jnp.zeros((N_HEADS, 1, 1), F32)
    for hd in range(N_HEADS):
        sink = jnp.where(head == hd, sink_ref[hd], sink)

    def sub_block(sb, carry):
        row0 = pl.multiple_of(sb * WINDOW, WINDOW)
        lo = jnp.where(first_tile & (sb == 0), WINDOW, 0)
        valid = band & (kj >= lo)
        scores = []
        for h in range(N_KV_HEADS):
            k_h = k_buf[pl.ds(row0, 2 * WINDOW), h * HEAD_DIM:(h + 1) * HEAD_DIM]
            q_h = jnp.concatenate(
                [q_ref[pl.ds(row0, WINDOW), (h * GROUP + g) * HEAD_DIM:(h * GROUP + g + 1) * HEAD_DIM]
                 for g in range(GROUP)], axis=0)
            scores.append(_dot_nt(q_h, k_h).reshape(GROUP, WINDOW, 2 * WINDOW))
        s = jnp.concatenate(scores, axis=0) * ATTN_SCALE
        p = _sink_softmax(s, valid[None], sink).astype(BF16)
        outs = []
        for h in range(N_KV_HEADS):
            v_h = v_buf[pl.ds(row0, 2 * WINDOW), h * HEAD_DIM:(h + 1) * HEAD_DIM]
            p_h = p[h * GROUP:(h + 1) * GROUP].reshape(GROUP * WINDOW, 2 * WINDOW)
            o_h = _dot(p_h, v_h)
            outs.extend(o_h[g * WINDOW:(g + 1) * WINDOW, :] for g in range(GROUP))
        o_ref[pl.ds(row0, WINDOW), :] = jnp.concatenate(outs, axis=-1).astype(BF16)
        return carry

    lax.fori_loop(0, ATT_SUB, sub_block, 0)


def _prompt_attn(sink, q, kv, batch, seq):
    tiles = seq // ATT_Q
    prev_idx = lambda b, i: (jnp.maximum((b * tiles + i) * ATT_SUB - 1, 0), 0)
    return pl.pallas_call(
        _prompt_attn_kernel,
        grid=(batch, tiles),
        in_specs=[
            pl.BlockSpec(memory_space=pltpu.SMEM),
            pl.BlockSpec((ATT_Q, ATT_W), lambda b, i: (b * tiles + i, 0)),
            pl.BlockSpec((ATT_Q, 2 * KV_W), lambda b, i: (b * tiles + i, 0)),
            pl.BlockSpec((WINDOW, 2 * KV_W), prev_idx),
        ],
        out_specs=pl.BlockSpec((ATT_Q, ATT_W), lambda b, i: (b * tiles + i, 0)),
        out_shape=jax.ShapeDtypeStruct((batch * seq, ATT_W), BF16),
        scratch_shapes=[pltpu.VMEM((WINDOW + ATT_Q, KV_W), BF16),
                        pltpu.VMEM((WINDOW + ATT_Q, KV_W), BF16)],
        compiler_params=_params(("arbitrary", "arbitrary")),
        name="prompt_attn",
    )(sink, q, kv, kv)


DEC_G = 8
KEY_PAD = 8


def _sample_attn_kernel(sink_ref, q_ref, kv_ref, ck_ref, cv_ref, o_ref, nk_ref, nv_ref,
                        *, dec_seq):
    rows = GROUP * dec_seq
    keys = WINDOW + KEY_PAD
    n_chain = DEC_G * N_KV_HEADS
    pad = jnp.zeros((KEY_PAD - dec_seq, KV_W), F32)

    scores, values = [], []
    for b in range(DEC_G):
        k_new = kv_ref[b, :, 0:KV_W]
        v_new = kv_ref[b, :, KV_W:2 * KV_W]
        nk_ref[b, 0:WINDOW - dec_seq, :] = ck_ref[b, dec_seq:WINDOW, :]
        nv_ref[b, 0:WINDOW - dec_seq, :] = cv_ref[b, dec_seq:WINDOW, :]
        nk_ref[b, WINDOW - dec_seq:WINDOW, :] = k_new
        nv_ref[b, WINDOW - dec_seq:WINDOW, :] = v_new
        k_all = jnp.concatenate([ck_ref[b], k_new, pad], axis=0).astype(BF16)
        v_all = jnp.concatenate([cv_ref[b], v_new, pad], axis=0).astype(BF16)
        for h in range(N_KV_HEADS):
            scores.append(_dot_nt(q_ref[b, h], k_all[:, h * HEAD_DIM:(h + 1) * HEAD_DIM]))
            values.append(v_all[:, h * HEAD_DIM:(h + 1) * HEAD_DIM])

    s = jnp.concatenate(scores, axis=0) * ATTN_SCALE
    r = lax.broadcasted_iota(jnp.int32, s.shape, 0)
    c = lax.broadcasted_iota(jnp.int32, s.shape, 1)
    t = r % dec_seq
    valid = ((c < WINDOW) & (c > t)) | ((c >= WINDOW) & (c - WINDOW <= t))
    head = (lax.broadcasted_iota(jnp.int32, (s.shape[0], 1), 0) // dec_seq) % N_HEADS
    sink_col = jnp.zeros((s.shape[0], 1), F32)
    for hd in range(N_HEADS):
        sink_col = jnp.where(head == hd, sink_ref[hd], sink_col)
    p = _sink_softmax(s, valid, sink_col).astype(BF16)

    for i in range(n_chain):
        o_ref[i // N_KV_HEADS, i % N_KV_HEADS] = _dot(p[i * rows:(i + 1) * rows, :], values[i])


def _sample_attn(sink, q, kv, ck, cv):
    nb, dec_seq, _ = kv.shape
    rows = GROUP * dec_seq
    blk3 = lambda d1, d2: pl.BlockSpec((DEC_G, d1, d2), lambda i: (i, 0, 0))
    blk4 = pl.BlockSpec((DEC_G, N_KV_HEADS, rows, HEAD_DIM), lambda i: (i, 0, 0, 0))
    return pl.pallas_call(
        functools.partial(_sample_attn_kernel, dec_seq=dec_seq),
        grid=(nb // DEC_G,),
        in_specs=[
            pl.BlockSpec(memory_space=pltpu.SMEM),
            blk4, blk3(dec_seq, 2 * KV_W), blk3(WINDOW, KV_W), blk3(WINDOW, KV_W),
        ],
        out_specs=[blk4, blk3(WINDOW, KV_W), blk3(WINDOW, KV_W)],
        out_shape=[
            jax.ShapeDtypeStruct((nb, N_KV_HEADS, rows, HEAD_DIM), F32),
            jax.ShapeDtypeStruct((nb, WINDOW, KV_W), F32),
            jax.ShapeDtypeStruct((nb, WINDOW, KV_W), F32),
        ],
        compiler_params=_params(("arbitrary",)),
        name="sample_attn",
    )(sink, q, kv, ck, cv)


def _to_head_major(q, nb, dec_seq):
    q = q.reshape(nb, dec_seq, N_KV_HEADS, GROUP, HEAD_DIM).transpose(0, 2, 3, 1, 4)
    return q.reshape(nb, N_KV_HEADS, GROUP * dec_seq, HEAD_DIM)


def _from_head_major(o, nb, dec_seq):
    o = o.reshape(nb, N_KV_HEADS, GROUP, dec_seq, HEAD_DIM).transpose(0, 3, 1, 2, 4)
    return o.reshape(nb * dec_seq, ATT_W)


def _sample_conv_kernel(st_ref, u_ref, w_ref, cb_ref, g_ref, b_ref, o_ref, ns_ref, buf, acc,
                        *, dec_seq):
    hist = CONV_W - 1
    buf[:, 0:hist, :] = st_ref[...]
    buf[:, hist:hist + dec_seq, :] = u_ref[...]
    ns_ref[...] = buf[:, dec_seq:dec_seq + hist, :]
    for b in range(DEC_G):
        a = jnp.zeros((dec_seq, D_CONV), F32)
        for j in range(CONV_W):
            a = a + w_ref[j:j + 1, :] * buf[b, j:j + dec_seq, :]
        acc[b] = a
    o_ref[...] = _ln_swish(acc[...] + cb_ref[...], g_ref[...], b_ref[...])


def _sample_conv(state, u, w, cb, g, b):
    nb, dec_seq, _ = u.shape
    hist = CONV_W - 1
    blk3 = lambda d1: pl.BlockSpec((DEC_G, d1, D_CONV), lambda i: (i, 0, 0))
    return pl.pallas_call(
        functools.partial(_sample_conv_kernel, dec_seq=dec_seq),
        grid=(nb // DEC_G,),
        in_specs=[blk3(hist), blk3(dec_seq), _resident((CONV_W, D_CONV)),
                  _resident((1, D_CONV)), _resident((1, D_CONV)), _resident((1, D_CONV))],
        out_specs=[blk3(dec_seq), blk3(hist)],
        out_shape=[jax.ShapeDtypeStruct((nb, dec_seq, D_CONV), F32),
                   jax.ShapeDtypeStruct((nb, hist, D_CONV), F32)],
        scratch_shapes=[pltpu.VMEM((DEC_G, hist + dec_seq, D_CONV), F32),
                        pltpu.VMEM((DEC_G, dec_seq, D_CONV), F32)],
        compiler_params=_params(("arbitrary",)),
        name="sample_conv",
    )(state, u, w, cb, g, b)


MIX_T = 256


def _mix_kernel(att_ref, ycp_ref, gate_ref, x_ref, wa_ref, wc_ref, wo_ref, g2_ref, h_ref, hn_ref):
    ya = _dot(att_ref[...], wa_ref[...])
    yc = _dot(ycp_ref[...], wc_ref[...])
    mixed = (gate_ref[:, 0:D_MODEL].astype(F32) * ya
             + gate_ref[:, D_MODEL:2 * D_MODEL].astype(F32) * yc)
    h = x_ref[...] + _dot(mixed.astype(BF16), wo_ref[...])
    h_ref[...] = h
    hn_ref[...] = _rms(h, g2_ref[...]).astype(BF16)


def _mix(att, ycp, gates, x, wa, wc, wo, g2):
    t = x.shape[0]
    row = lambda w: pl.BlockSpec((MIX_T, w), lambda i: (i, 0))
    return pl.pallas_call(
        _mix_kernel,
        grid=(t // MIX_T,),
        in_specs=[row(ATT_W), row(D_CONV), row(N_BRANCH * D_MODEL), row(D_MODEL),
                  _resident((ATT_W, D_MODEL)), _resident((D_CONV, D_MODEL)),
                  _resident((D_MODEL, D_MODEL)), _resident((1, D_MODEL))],
        out_specs=[row(D_MODEL), row(D_MODEL)],
        out_shape=[jax.ShapeDtypeStruct((t, D_MODEL), F32),
                   jax.ShapeDtypeStruct((t, D_MODEL), BF16)],
        compiler_params=_params(("arbitrary",)),
        name="mix",
    )(att, ycp, gates, x, wa, wc, wo, g2)


MLP_T = 512
MLP_F = 1024


def _mlp_kernel(hn_ref, h_ref, wu_ref, wd_ref, gf_ref, y_ref, *, final_norm):
    f = pl.program_id(1)

    @pl.when(f == 0)
    def _():
        y_ref[...] = h_ref[...]

    z = jnp.maximum(_dot(hn_ref[...], wu_ref[...]), 0.0)
    y_ref[...] += _dot((z * z).astype(BF16), wd_ref[...])

    if final_norm:
        @pl.when(f == pl.num_programs(1) - 1)
        def _():
            y_ref[...] = _rms(y_ref[...], gf_ref[...])


def _mlp(hn, h, wu, wd, gf, final_norm):
    t = h.shape[0]
    return pl.pallas_call(
        functools.partial(_mlp_kernel, final_norm=final_norm),
        grid=(t // MLP_T, D_FF // MLP_F),
        in_specs=[
            pl.BlockSpec((MLP_T, D_MODEL), lambda i, f: (i, 0)),
            pl.BlockSpec((MLP_T, D_MODEL), lambda i, f: (i, 0)),
            pl.BlockSpec((D_MODEL, MLP_F), lambda i, f: (0, f)),
            pl.BlockSpec((MLP_F, D_MODEL), lambda i, f: (f, 0)),
            _resident((1, D_MODEL)),
        ],
        out_specs=pl.BlockSpec((MLP_T, D_MODEL), lambda i, f: (i, 0)),
        out_shape=jax.ShapeDtypeStruct((t, D_MODEL), F32),
        compiler_params=_params(("arbitrary", "arbitrary")),
        name="mlp",
    )(hn, h, wu, wd, gf)


def _split_w_in(w_in, b_gate):
    o_k = ATT_W
    o_u = ATT_W + 2 * KV_W
    o_g = o_u + 2 * D_CONV
    wb = w_in.astype(BF16)
    return (wb[:, :o_k], wb[:, o_k:o_u], wb[:, o_u:o_u + D_CONV], wb[:, o_u + D_CONV:o_g],
            wb[:, o_g:], b_gate.reshape(1, -1))


def kernel(x_prompt, x_sample, cache_k, cache_v, state_conv, norm1_g, w_in, b_gate, sink,
           w_attn_o, conv_w, conv_b, cln_g, cln_b, w_conv_o, w_out, norm2_g, w_up, w_down,
           norm_f_g):
    depth = w_in.shape[0]
    batch, seq, _ = x_prompt.shape
    nb, dec_seq, _ = x_sample.shape
    hist = CONV_W - 1
    hp = x_prompt.reshape(batch * seq, D_MODEL)
    hs = x_sample.reshape(nb * dec_seq, D_MODEL)
    row = lambda v: v.reshape(1, -1)
    gf = row(norm_f_g)
    outs = [[] for _ in range(6)]
    for l in range(depth):
        last = l == depth - 1
        wq, wkv, wua, wub, wg, bg = _split_w_in(w_in[l], b_gate[l])
        wa, wc, wo = w_attn_o[l].astype(BF16), w_conv_o[l].astype(BF16), w_out[l].astype(BF16)
        wu, wd = w_up[l].astype(BF16), w_down[l].astype(BF16)
        g1, g2 = row(norm1_g[l]), row(norm2_g[l])
        cb, lg, lb = row(conv_b[l]), row(cln_g[l]), row(cln_b[l])

        q, kv, gates, ycp, u_tail = _proj_conv(hp, g1, wq, wkv, wua, wub, wg, bg,
                                               conv_w[l], cb, lg, lb, seq)
        att = _prompt_attn(sink[l], q, kv, batch, seq)
        h, hn = _mix(att, ycp, gates, hp, wa, wc, wo, g2)
        hp = _mlp(hn, h, wu, wd, gf, final_norm=last)
        kv3 = kv.reshape(batch, seq, 2 * KV_W)
        outs[0].append(kv3[:, seq - WINDOW:, :KV_W].reshape(batch, WINDOW, N_KV_HEADS, HEAD_DIM))
        outs[1].append(kv3[:, seq - WINDOW:, KV_W:].reshape(batch, WINDOW, N_KV_HEADS, HEAD_DIM))
        u_tail = u_tail.reshape(batch, seq // PROJ_T, CONV_HALO, D_CONV)
        outs[2].append(u_tail[:, -1, CONV_HALO - hist:, :])

        q, kv, u, gates = _proj(hs, g1, wq, wkv, wua, wub, wg, bg)
        att, nk, nv = _sample_attn(
            sink[l], _to_head_major(q, nb, dec_seq), kv.reshape(nb, dec_seq, 2 * KV_W),
            cache_k[l].reshape(nb, WINDOW, KV_W), cache_v[l].reshape(nb, WINDOW, KV_W))
        ycp, ns = _sample_conv(state_conv[l], u.reshape(nb, dec_seq, D_CONV), conv_w[l], cb, lg, lb)
        h, hn = _mix(_from_head_major(att, nb, dec_seq).astype(BF16),
                     ycp.reshape(nb * dec_seq, D_CONV).astype(BF16), gates, hs, wa, wc, wo, g2)
        hs = _mlp(hn, h, wu, wd, gf, final_norm=last)
        outs[3].append(nk.reshape(nb, WINDOW, N_KV_HEADS, HEAD_DIM))
        outs[4].append(nv.reshape(nb, WINDOW, N_KV_HEADS, HEAD_DIM))
        outs[5].append(ns)

    return (hp.reshape(batch, seq, D_MODEL), hs.reshape(nb, dec_seq, D_MODEL),
            *[jnp.stack(o) for o in outs])
```

```python
import functools

import jax
import jax.numpy as jnp
from jax import lax
from jax.experimental import pallas as pl
from jax.experimental.pallas import tpu as pltpu

D_MODEL = 2048
HEAD_DIM = 64
N_HEADS = 16
N_KV_HEADS = 4
GROUP = N_HEADS // N_KV_HEADS
ATT_W = N_HEADS * HEAD_DIM
KV_W = N_KV_HEADS * HEAD_DIM
WINDOW = 128
D_CONV = D_MODEL // 2
CONV_W = 31
N_BRANCH = 2
D_FF = 4 * D_MODEL
EPS = 1e-6
NEG = -1e30
ATTN_SCALE = HEAD_DIM ** -0.5

F32 = jnp.float32
BF16 = jnp.bfloat16

V7X_VMEM_LIMIT_BYTES = 58 * 1024 * 1024


def _dot(a, b):
    return jnp.dot(a, b, preferred_element_type=F32)


def _dot_nt(a, b):
    return lax.dot_general(a, b, (((1,), (1,)), ((), ())), preferred_element_type=F32)


def _sigmoid(x):
    return 1.0 / (1.0 + jnp.exp(-x))


def _rms(x, g):
    return x * lax.rsqrt(jnp.mean(x * x, axis=-1, keepdims=True) + EPS) * g


def _params(sem):
    return pltpu.CompilerParams(dimension_semantics=sem, vmem_limit_bytes=V7X_VMEM_LIMIT_BYTES)


def _resident(shape):
    return pl.BlockSpec(shape, lambda *_: (0,) * len(shape), pipeline_mode=pl.Buffered(1))


PROJ_T = 256
CONV_HALO = 32
CONV_ROWS = 64
LANES = 128


def _ln_swish(c, g, b):
    mu = jnp.mean(c, axis=-1, keepdims=True)
    xc = c - mu
    y = xc * lax.rsqrt(jnp.mean(xc * xc, axis=-1, keepdims=True) + EPS) * g + b
    return y * _sigmoid(y)


def _proj_kernel(x_ref, g1_ref, wq_ref, wkv_ref, wua_ref, wub_ref, wg_ref, bg_ref,
                 q_ref, kv_ref, u_ref, gate_ref):
    xn = _rms(x_ref[...], g1_ref[...]).astype(BF16)
    q_ref[...] = (_dot(xn, wq_ref[...]) * ATTN_SCALE).astype(BF16)
    kv_ref[...] = _dot(xn, wkv_ref[...])
    u_ref[...] = _dot(xn, wua_ref[...]) * _sigmoid(_dot(xn, wub_ref[...]))
    gate_ref[...] = _sigmoid(_dot(xn, wg_ref[...]) + bg_ref[...]).astype(BF16)


def _proj_conv_kernel(x_ref, g1_ref, wq_ref, wkv_ref, wua_ref, wub_ref, wg_ref, bg_ref,
                      cw_ref, cb_ref, lg_ref, lb_ref,
                      q_ref, kv_ref, gate_ref, ycp_ref, tail_ref, buf, acc, *, tiles_per_seq):
    first = pl.program_id(0) % tiles_per_seq == 0

    @pl.when(first)
    def _():
        buf[:, 0:CONV_HALO, :] = jnp.zeros((D_CONV // LANES, CONV_HALO, LANES), F32)

    @pl.when(jnp.logical_not(first))
    def _():
        buf[:, 0:CONV_HALO, :] = buf[:, PROJ_T:PROJ_T + CONV_HALO, :]

    xn = _rms(x_ref[...], g1_ref[...]).astype(BF16)
    q_ref[...] = (_dot(xn, wq_ref[...]) * ATTN_SCALE).astype(BF16)
    kv_ref[...] = _dot(xn, wkv_ref[...])
    u = _dot(xn, wua_ref[...]) * _sigmoid(_dot(xn, wub_ref[...]))
    tail_ref[...] = u[PROJ_T - CONV_HALO:, :]
    for c in range(D_CONV // LANES):
        buf[c, CONV_HALO:CONV_HALO + PROJ_T, :] = u[:, c * LANES:(c + 1) * LANES]
    gate_ref[...] = _sigmoid(_dot(xn, wg_ref[...]) + bg_ref[...]).astype(BF16)
    off = CONV_HALO - (CONV_W - 1)
    for c in range(D_CONV // LANES):
        lanes = slice(c * LANES, (c + 1) * LANES)
        for r0 in range(0, PROJ_T, CONV_ROWS):
            a = jnp.zeros((CONV_ROWS, LANES), F32)
            for j in range(CONV_W):
                a = a + cw_ref[j:j + 1, lanes] * buf[c, r0 + off + j:r0 + off + j + CONV_ROWS, :]
            acc[r0:r0 + CONV_ROWS, lanes] = a
    ycp_ref[...] = _ln_swish(acc[...] + cb_ref[...], lg_ref[...], lb_ref[...]).astype(BF16)


def _proj_specs():
    return [
        pl.BlockSpec((PROJ_T, D_MODEL), lambda i: (i, 0)),
        _resident((1, D_MODEL)),
        _resident((D_MODEL, ATT_W)),
        _resident((D_MODEL, 2 * KV_W)),
        _resident((D_MODEL, D_CONV)),
        _resident((D_MODEL, D_CONV)),
        _resident((D_MODEL, N_BRANCH * D_MODEL)),
        _resident((1, N_BRANCH * D_MODEL)),
    ]


def _proj_conv(x, g1, wq, wkv, wua, wub, wg, bg, cw, cb, lg, lb, seq):
    t = x.shape[0]
    tiles = t // PROJ_T
    row = lambda w: pl.BlockSpec((PROJ_T, w), lambda i: (i, 0))
    return pl.pallas_call(
        functools.partial(_proj_conv_kernel, tiles_per_seq=seq // PROJ_T),
        grid=(tiles,),
        in_specs=_proj_specs() + [_resident((CONV_W, D_CONV)), _resident((1, D_CONV)),
                                  _resident((1, D_CONV)), _resident((1, D_CONV))],
        out_specs=[row(ATT_W), row(2 * KV_W), row(N_BRANCH * D_MODEL), row(D_CONV),
                   pl.BlockSpec((CONV_HALO, D_CONV), lambda i: (i, 0))],
        out_shape=[
            jax.ShapeDtypeStruct((t, ATT_W), BF16),
            jax.ShapeDtypeStruct((t, 2 * KV_W), F32),
            jax.ShapeDtypeStruct((t, N_BRANCH * D_MODEL), BF16),
            jax.ShapeDtypeStruct((t, D_CONV), BF16),
            jax.ShapeDtypeStruct((tiles * CONV_HALO, D_CONV), F32),
        ],
        scratch_shapes=[pltpu.VMEM((D_CONV // LANES, CONV_HALO + PROJ_T, LANES), F32),
                        pltpu.VMEM((PROJ_T, D_CONV), F32)],
        compiler_params=_params(("arbitrary",)),
        name="proj_conv",
    )(x, g1, wq, wkv, wua, wub, wg, bg, cw, cb, lg, lb)


def _proj(x, g1, wq, wkv, wua, wub, wg, bg):
    t = x.shape[0]
    row = lambda w: pl.BlockSpec((PROJ_T, w), lambda i: (i, 0))
    return pl.pallas_call(
        _proj_kernel,
        grid=(t // PROJ_T,),
        in_specs=_proj_specs(),
        out_specs=[row(ATT_W), row(2 * KV_W), row(D_CONV), row(N_BRANCH * D_MODEL)],
        out_shape=[
            jax.ShapeDtypeStruct((t, ATT_W), BF16),
            jax.ShapeDtypeStruct((t, 2 * KV_W), F32),
            jax.ShapeDtypeStruct((t, D_CONV), F32),
            jax.ShapeDtypeStruct((t, N_BRANCH * D_MODEL), BF16),
        ],
        compiler_params=_params(("arbitrary",)),
        name="proj",
    )(x, g1, wq, wkv, wua, wub, wg, bg)


def _sink_softmax(s, valid, sink):
    s = jnp.where(valid, s, NEG)
    m = jnp.maximum(jnp.max(s, axis=-1, keepdims=True), sink)
    e = jnp.exp(s - m)
    den = jnp.sum(e, axis=-1, keepdims=True) + jnp.exp(sink - m)
    return e, 1.0 / den


ATT_Q = 512
ATT_SUB = ATT_Q // WINDOW


def _prompt_attn_kernel(sink_ref, q_ref, kvc_ref, kvp_ref, o_ref, k_buf, v_buf):
    first_tile = pl.program_id(1) == 0
    k_buf[0:WINDOW, :] = kvp_ref[:, 0:KV_W].astype(BF16)
    v_buf[0:WINDOW, :] = kvp_ref[:, KV_W:2 * KV_W].astype(BF16)
    k_buf[WINDOW:WINDOW + ATT_Q, :] = kvc_ref[:, 0:KV_W].astype(BF16)
    v_buf[WINDOW:WINDOW + ATT_Q, :] = kvc_ref[:, KV_W:2 * KV_W].astype(BF16)

    qi = lax.broadcasted_iota(jnp.int32, (WINDOW, 2 * WINDOW), 0)
    kj = lax.broadcasted_iota(jnp.int32, (WINDOW, 2 * WINDOW), 1)
    band = (kj > qi) & (kj <= qi + WINDOW)
    head = lax.broadcasted_iota(jnp.int32, (N_HEADS, 1, 1), 0)
    sink = jnp.zeros((N_HEADS, 1, 1), F32)
    for hd in range(N_HEADS):
        sink = jnp.where(head == hd, sink_ref[hd], sink)

    def sub_block(sb, carry):
        row0 = pl.multiple_of(sb * WINDOW, WINDOW)
        lo = jnp.where(first_tile & (sb == 0), WINDOW, 0)
        valid = band & (kj >= lo)
        scores = []
        for h in range(N_KV_HEADS):
            k_h = k_buf[pl.ds(row0, 2 * WINDOW), h * HEAD_DIM:(h + 1) * HEAD_DIM]
            q_h = jnp.concatenate(
                [q_ref[pl.ds(row0, WINDOW), (h * GROUP + g) * HEAD_DIM:(h * GROUP + g + 1) * HEAD_DIM]
                 for g in range(GROUP)], axis=0)
            scores.append(_dot_nt(q_h, k_h).reshape(GROUP, WINDOW, 2 * WINDOW))
        s = jnp.concatenate(scores, axis=0)
        e, inv = _sink_softmax(s, valid[None], sink)
        e = e.astype(BF16)
        outs = []
        for h in range(N_KV_HEADS):
            v_h = v_buf[pl.ds(row0, 2 * WINDOW), h * HEAD_DIM:(h + 1) * HEAD_DIM]
            e_h = e[h * GROUP:(h + 1) * GROUP].reshape(GROUP * WINDOW, 2 * WINDOW)
            o_h = _dot(e_h, v_h) * inv[h * GROUP:(h + 1) * GROUP].reshape(GROUP * WINDOW, 1)
            outs.extend(o_h[g * WINDOW:(g + 1) * WINDOW, :] for g in range(GROUP))
        o_ref[pl.ds(row0, WINDOW), :] = jnp.concatenate(outs, axis=-1).astype(BF16)
        return carry

    lax.fori_loop(0, ATT_SUB, sub_block, 0)


def _prompt_attn(sink, q, kv, batch, seq):
    tiles = seq // ATT_Q
    prev_idx = lambda b, i: (jnp.maximum((b * tiles + i) * ATT_SUB - 1, 0), 0)
    return pl.pallas_call(
        _prompt_attn_kernel,
        grid=(batch, tiles),
        in_specs=[
            pl.BlockSpec(memory_space=pltpu.SMEM),
            pl.BlockSpec((ATT_Q, ATT_W), lambda b, i: (b * tiles + i, 0)),
            pl.BlockSpec((ATT_Q, 2 * KV_W), lambda b, i: (b * tiles + i, 0)),
            pl.BlockSpec((WINDOW, 2 * KV_W), prev_idx),
        ],
        out_specs=pl.BlockSpec((ATT_Q, ATT_W), lambda b, i: (b * tiles + i, 0)),
        out_shape=jax.ShapeDtypeStruct((batch * seq, ATT_W), BF16),
        scratch_shapes=[pltpu.VMEM((WINDOW + ATT_Q, KV_W), BF16),
                        pltpu.VMEM((WINDOW + ATT_Q, KV_W), BF16)],
        compiler_params=_params(("arbitrary", "arbitrary")),
        name="prompt_attn",
    )(sink, q, kv, kv)


DEC_G = 8
KEY_PAD = 8


def _sample_attn_kernel(sink_ref, q_ref, kv_ref, ck_ref, cv_ref, o_ref, nk_ref, nv_ref,
                        *, dec_seq):
    rows = GROUP * dec_seq
    keys = WINDOW + KEY_PAD
    n_chain = DEC_G * N_KV_HEADS
    pad = jnp.zeros((KEY_PAD - dec_seq, KV_W), F32)

    scores, values = [], []
    for b in range(DEC_G):
        k_new = kv_ref[b, :, 0:KV_W]
        v_new = kv_ref[b, :, KV_W:2 * KV_W]
        nk_ref[b, 0:WINDOW - dec_seq, :] = ck_ref[b, dec_seq:WINDOW, :]
        nv_ref[b, 0:WINDOW - dec_seq, :] = cv_ref[b, dec_seq:WINDOW, :]
        nk_ref[b, WINDOW - dec_seq:WINDOW, :] = k_new
        nv_ref[b, WINDOW - dec_seq:WINDOW, :] = v_new
        k_all = jnp.concatenate([ck_ref[b], k_new, pad], axis=0).astype(BF16)
        v_all = jnp.concatenate([cv_ref[b], v_new, pad], axis=0).astype(BF16)
        for h in range(N_KV_HEADS):
            scores.append(_dot_nt(q_ref[b, h], k_all[:, h * HEAD_DIM:(h + 1) * HEAD_DIM]))
            values.append(v_all[:, h * HEAD_DIM:(h + 1) * HEAD_DIM])

    s = jnp.concatenate(scores, axis=0)
    r = lax.broadcasted_iota(jnp.int32, s.shape, 0)
    c = lax.broadcasted_iota(jnp.int32, s.shape, 1)
    t = r % dec_seq
    valid = ((c < WINDOW) & (c > t)) | ((c >= WINDOW) & (c - WINDOW <= t))
    head = (lax.broadcasted_iota(jnp.int32, (s.shape[0], 1), 0) // dec_seq) % N_HEADS
    sink_col = jnp.zeros((s.shape[0], 1), F32)
    for hd in range(N_HEADS):
        sink_col = jnp.where(head == hd, sink_ref[hd], sink_col)
    e, inv = _sink_softmax(s, valid, sink_col)
    e = e.astype(BF16)

    for i in range(n_chain):
        sl = slice(i * rows, (i + 1) * rows)
        o_ref[i // N_KV_HEADS, i % N_KV_HEADS] = _dot(e[sl, :], values[i]) * inv[sl, :]


def _sample_attn(sink, q, kv, ck, cv):
    nb, dec_seq, _ = kv.shape
    rows = GROUP * dec_seq
    blk3 = lambda d1, d2: pl.BlockSpec((DEC_G, d1, d2), lambda i: (i, 0, 0))
    blk4 = pl.BlockSpec((DEC_G, N_KV_HEADS, rows, HEAD_DIM), lambda i: (i, 0, 0, 0))
    return pl.pallas_call(
        functools.partial(_sample_attn_kernel, dec_seq=dec_seq),
        grid=(nb // DEC_G,),
        in_specs=[
            pl.BlockSpec(memory_space=pltpu.SMEM),
            blk4, blk3(dec_seq, 2 * KV_W), blk3(WINDOW, KV_W), blk3(WINDOW, KV_W),
        ],
        out_specs=[blk4, blk3(WINDOW, KV_W), blk3(WINDOW, KV_W)],
        out_shape=[
            jax.ShapeDtypeStruct((nb, N_KV_HEADS, rows, HEAD_DIM), F32),
            jax.ShapeDtypeStruct((nb, WINDOW, KV_W), F32),
            jax.ShapeDtypeStruct((nb, WINDOW, KV_W), F32),
        ],
        compiler_params=_params(("arbitrary",)),
        name="sample_attn",
    )(sink, q, kv, ck, cv)


def _to_head_major(q, nb, dec_seq):
    q = q.reshape(nb, dec_seq, N_KV_HEADS, GROUP, HEAD_DIM).transpose(0, 2, 3, 1, 4)
    return q.reshape(nb, N_KV_HEADS, GROUP * dec_seq, HEAD_DIM)


def _from_head_major(o, nb, dec_seq):
    o = o.reshape(nb, N_KV_HEADS, GROUP, dec_seq, HEAD_DIM).transpose(0, 3, 1, 2, 4)
    return o.reshape(nb * dec_seq, ATT_W)


def _sample_conv_kernel(st_ref, u_ref, w_ref, cb_ref, g_ref, b_ref, o_ref, ns_ref, buf, acc,
                        *, dec_seq):
    hist = CONV_W - 1
    buf[:, 0:hist, :] = st_ref[...]
    buf[:, hist:hist + dec_seq, :] = u_ref[...]
    ns_ref[...] = buf[:, dec_seq:dec_seq + hist, :]
    for b in range(DEC_G):
        a = jnp.zeros((dec_seq, D_CONV), F32)
        for j in range(CONV_W):
            a = a + w_ref[j:j + 1, :] * buf[b, j:j + dec_seq, :]
        acc[b] = a
    o_ref[...] = _ln_swish(acc[...] + cb_ref[...], g_ref[...], b_ref[...])


def _sample_conv(state, u, w, cb, g, b):
    nb, dec_seq, _ = u.shape
    hist = CONV_W - 1
    blk3 = lambda d1: pl.BlockSpec((DEC_G, d1, D_CONV), lambda i: (i, 0, 0))
    return pl.pallas_call(
        functools.partial(_sample_conv_kernel, dec_seq=dec_seq),
        grid=(nb // DEC_G,),
        in_specs=[blk3(hist), blk3(dec_seq), _resident((CONV_W, D_CONV)),
                  _resident((1, D_CONV)), _resident((1, D_CONV)), _resident((1, D_CONV))],
        out_specs=[blk3(dec_seq), blk3(hist)],
        out_shape=[jax.ShapeDtypeStruct((nb, dec_seq, D_CONV), F32),
                   jax.ShapeDtypeStruct((nb, hist, D_CONV), F32)],
        scratch_shapes=[pltpu.VMEM((DEC_G, hist + dec_seq, D_CONV), F32),
                        pltpu.VMEM((DEC_G, dec_seq, D_CONV), F32)],
        compiler_params=_params(("arbitrary",)),
        name="sample_conv",
    )(state, u, w, cb, g, b)


MIX_T = 256


def _mix_kernel(att_ref, ycp_ref, gate_ref, x_ref, wa_ref, wc_ref, wo_ref, g2_ref, h_ref, hn_ref):
    ya = _dot(att_ref[...], wa_ref[...])
    yc = _dot(ycp_ref[...], wc_ref[...])
    mixed = (gate_ref[:, 0:D_MODEL].astype(F32) * ya
             + gate_ref[:, D_MODEL:2 * D_MODEL].astype(F32) * yc)
    h = x_ref[...] + _dot(mixed.astype(BF16), wo_ref[...])
    h_ref[...] = h
    hn_ref[...] = _rms(h, g2_ref[...]).astype(BF16)


def _mix(att, ycp, gates, x, wa, wc, wo, g2):
    t = x.shape[0]
    row = lambda w: pl.BlockSpec((MIX_T, w), lambda i: (i, 0))
    return pl.pallas_call(
        _mix_kernel,
        grid=(t // MIX_T,),
        in_specs=[row(ATT_W), row(D_CONV), row(N_BRANCH * D_MODEL), row(D_MODEL),
                  _resident((ATT_W, D_MODEL)), _resident((D_CONV, D_MODEL)),
                  _resident((D_MODEL, D_MODEL)), _resident((1, D_MODEL))],
        out_specs=[row(D_MODEL), row(D_MODEL)],
        out_shape=[jax.ShapeDtypeStruct((t, D_MODEL), F32),
                   jax.ShapeDtypeStruct((t, D_MODEL), BF16)],
        compiler_params=_params(("arbitrary",)),
        name="mix",
    )(att, ycp, gates, x, wa, wc, wo, g2)


MLP_T = 512
MLP_F = 1024


def _mlp_kernel(hn_ref, h_ref, wu_ref, wd_ref, gf_ref, y_ref, *, final_norm):
    f = pl.program_id(1)

    @pl.when(f == 0)
    def _():
        y_ref[...] = h_ref[...]

    z = jnp.maximum(_dot(hn_ref[...], wu_ref[...]), 0.0)
    y_ref[...] += _dot((z * z).astype(BF16), wd_ref[...])

    if final_norm:
        @pl.when(f == pl.num_programs(1) - 1)
        def _():
            y_ref[...] = _rms(y_ref[...], gf_ref[...])


def _mlp(hn, h, wu, wd, gf, final_norm):
    t = h.shape[0]
    return pl.pallas_call(
        functools.partial(_mlp_kernel, final_norm=final_norm),
        grid=(t // MLP_T, D_FF // MLP_F),
        in_specs=[
            pl.BlockSpec((MLP_T, D_MODEL), lambda i, f: (i, 0)),
            pl.BlockSpec((MLP_T, D_MODEL), lambda i, f: (i, 0)),
            pl.BlockSpec((D_MODEL, MLP_F), lambda i, f: (0, f)),
            pl.BlockSpec((MLP_F, D_MODEL), lambda i, f: (f, 0)),
            _resident((1, D_MODEL)),
        ],
        out_specs=pl.BlockSpec((MLP_T, D_MODEL), lambda i, f: (i, 0)),
        out_shape=jax.ShapeDtypeStruct((t, D_MODEL), F32),
        compiler_params=_params(("arbitrary", "arbitrary")),
        name="mlp",
    )(hn, h, wu, wd, gf)


def _split_w_in(w_in, b_gate):
    o_k = ATT_W
    o_u = ATT_W + 2 * KV_W
    o_g = o_u + 2 * D_CONV
    cols = ((0, o_k), (o_k, o_u), (o_u, o_u + D_CONV), (o_u + D_CONV, o_g), (o_g, w_in.shape[1]))
    return (*[w_in[:, a:b].astype(BF16) for a, b in cols], b_gate.reshape(1, -1))


def kernel(x_prompt, x_sample, cache_k, cache_v, state_conv, norm1_g, w_in, b_gate, sink,
           w_attn_o, conv_w, conv_b, cln_g, cln_b, w_conv_o, w_out, norm2_g, w_up, w_down,
           norm_f_g):
    depth = w_in.shape[0]
    batch, seq, _ = x_prompt.shape
    nb, dec_seq, _ = x_sample.shape
    hist = CONV_W - 1
    hp = x_prompt.reshape(batch * seq, D_MODEL)
    hs = x_sample.reshape(nb * dec_seq, D_MODEL)
    row = lambda v: v.reshape(1, -1)
    gf = row(norm_f_g)
    outs = [[] for _ in range(6)]
    for l in range(depth):
        last = l == depth - 1
        wq, wkv, wua, wub, wg, bg = _split_w_in(w_in[l], b_gate[l])
        wa, wc, wo = w_attn_o[l].astype(BF16), w_conv_o[l].astype(BF16), w_out[l].astype(BF16)
        wu, wd = w_up[l].astype(BF16), w_down[l].astype(BF16)
        g1, g2 = row(norm1_g[l]), row(norm2_g[l])
        cb, lg, lb = row(conv_b[l]), row(cln_g[l]), row(cln_b[l])

        q, kv, gates, ycp, u_tail = _proj_conv(hp, g1, wq, wkv, wua, wub, wg, bg,
                                               conv_w[l], cb, lg, lb, seq)
        att = _prompt_attn(sink[l], q, kv, batch, seq)
        h, hn = _mix(att, ycp, gates, hp, wa, wc, wo, g2)
        hp = _mlp(hn, h, wu, wd, gf, final_norm=last)
        kv3 = kv.reshape(batch, seq, 2 * KV_W)
        outs[0].append(kv3[:, seq - WINDOW:, :KV_W].reshape(batch, WINDOW, N_KV_HEADS, HEAD_DIM))
        outs[1].append(kv3[:, seq - WINDOW:, KV_W:].reshape(batch, WINDOW, N_KV_HEADS, HEAD_DIM))
        u_tail = u_tail.reshape(batch, seq // PROJ_T, CONV_HALO, D_CONV)
        outs[2].append(u_tail[:, -1, CONV_HALO - hist:, :])

        q, kv, u, gates = _proj(hs, g1, wq, wkv, wua, wub, wg, bg)
        att, nk, nv = _sample_attn(
            sink[l], _to_head_major(q, nb, dec_seq), kv.reshape(nb, dec_seq, 2 * KV_W),
            cache_k[l].reshape(nb, WINDOW, KV_W), cache_v[l].reshape(nb, WINDOW, KV_W))
        ycp, ns = _sample_conv(state_conv[l], u.reshape(nb, dec_seq, D_CONV), conv_w[l], cb, lg, lb)
        h, hn = _mix(_from_head_major(att, nb, dec_seq).astype(BF16),
                     ycp.reshape(nb * dec_seq, D_CONV).astype(BF16), gates, hs, wa, wc, wo, g2)
        hs = _mlp(hn, h, wu, wd, gf, final_norm=last)
        outs[3].append(nk.reshape(nb, WINDOW, N_KV_HEADS, HEAD_DIM))
        outs[4].append(nv.reshape(nb, WINDOW, N_KV_HEADS, HEAD_DIM))
        outs[5].append(ns)

    return (hp.reshape(batch, seq, D_MODEL), hs.reshape(nb, dec_seq, D_MODEL),
            *[jnp.stack(o) for o in outs])
```

```python
import functools

import jax
import jax.numpy as jnp
from jax import lax
from jax.experimental import pallas as pl
from jax.experimental.pallas import tpu as pltpu

D_MODEL = 2048
HEAD_DIM = 64
N_HEADS = 16
N_KV_HEADS = 4
GROUP = N_HEADS // N_KV_HEADS
ATT_W = N_HEADS * HEAD_DIM
KV_W = N_KV_HEADS * HEAD_DIM
WINDOW = 128
D_CONV = D_MODEL // 2
CONV_W = 31
N_BRANCH = 2
D_FF = 4 * D_MODEL
EPS = 1e-6
NEG = -1e30
ATTN_SCALE = HEAD_DIM ** -0.5

F32 = jnp.float32
BF16 = jnp.bfloat16

V7X_VMEM_LIMIT_BYTES = 58 * 1024 * 1024


def _dot(a, b):
    return jnp.dot(a, b, preferred_element_type=F32)


def _dot_nt(a, b):
    return lax.dot_general(a, b, (((1,), (1,)), ((), ())), preferred_element_type=F32)


def _sigmoid(x):
    return 1.0 / (1.0 + jnp.exp(-x))


def _rms(x, g):
    return x * lax.rsqrt(jnp.mean(x * x, axis=-1, keepdims=True) + EPS) * g


def _params(sem):
    return pltpu.CompilerParams(dimension_semantics=sem, vmem_limit_bytes=V7X_VMEM_LIMIT_BYTES)


def _resident(shape):
    return pl.BlockSpec(shape, lambda *_: (0,) * len(shape), pipeline_mode=pl.Buffered(1))


PROJ_T = 256
CONV_HALO = 32
CONV_ROWS = 64
LANES = 128


def _ln_swish(c, g, b):
    mu = jnp.mean(c, axis=-1, keepdims=True)
    xc = c - mu
    y = xc * lax.rsqrt(jnp.mean(xc * xc, axis=-1, keepdims=True) + EPS) * g + b
    return y * _sigmoid(y)


def _proj_kernel(x_ref, g1_ref, wq_ref, wkv_ref, wua_ref, wub_ref, wg_ref, bg_ref,
                 q_ref, kv_ref, u_ref, gate_ref):
    xn = _rms(x_ref[...], g1_ref[...]).astype(BF16)
    q_ref[...] = (_dot(xn, wq_ref[...]) * ATTN_SCALE).astype(BF16)
    kv_ref[...] = _dot(xn, wkv_ref[...])
    u_ref[...] = _dot(xn, wua_ref[...]) * _sigmoid(_dot(xn, wub_ref[...]))
    gate_ref[...] = _sigmoid(_dot(xn, wg_ref[...]) + bg_ref[...]).astype(BF16)


def _proj_conv_kernel(x_ref, g1_ref, wq_ref, wkv_ref, wua_ref, wub_ref, wg_ref, bg_ref,
                      cw_ref, cb_ref, lg_ref, lb_ref,
                      q_ref, kv_ref, gate_ref, ycp_ref, tail_ref, buf, acc, *, tiles_per_seq):
    first = pl.program_id(0) % tiles_per_seq == 0

    @pl.when(first)
    def _():
        buf[:, 0:CONV_HALO, :] = jnp.zeros((D_CONV // LANES, CONV_HALO, LANES), F32)

    @pl.when(jnp.logical_not(first))
    def _():
        buf[:, 0:CONV_HALO, :] = buf[:, PROJ_T:PROJ_T + CONV_HALO, :]

    xn = _rms(x_ref[...], g1_ref[...]).astype(BF16)
    q_ref[...] = (_dot(xn, wq_ref[...]) * ATTN_SCALE).astype(BF16)
    kv_ref[...] = _dot(xn, wkv_ref[...])
    u = _dot(xn, wua_ref[...]) * _sigmoid(_dot(xn, wub_ref[...]))
    tail_ref[...] = u[PROJ_T - CONV_HALO:, :]
    for c in range(D_CONV // LANES):
        buf[c, CONV_HALO:CONV_HALO + PROJ_T, :] = u[:, c * LANES:(c + 1) * LANES]
    gate_ref[...] = _sigmoid(_dot(xn, wg_ref[...]) + bg_ref[...]).astype(BF16)
    off = CONV_HALO - (CONV_W - 1)
    for c in range(D_CONV // LANES):
        lanes = slice(c * LANES, (c + 1) * LANES)
        for r0 in range(0, PROJ_T, CONV_ROWS):
            a = jnp.zeros((CONV_ROWS, LANES), F32)
            for j in range(CONV_W):
                a = a + cw_ref[j:j + 1, lanes] * buf[c, r0 + off + j:r0 + off + j + CONV_ROWS, :]
            acc[r0:r0 + CONV_ROWS, lanes] = a
    ycp_ref[...] = _ln_swish(acc[...] + cb_ref[...], lg_ref[...], lb_ref[...]).astype(BF16)


def _proj_specs():
    return [
        pl.BlockSpec((PROJ_T, D_MODEL), lambda i: (i, 0)),
        _resident((1, D_MODEL)),
        _resident((D_MODEL, ATT_W)),
        _resident((D_MODEL, 2 * KV_W)),
        _resident((D_MODEL, D_CONV)),
        _resident((D_MODEL, D_CONV)),
        _resident((D_MODEL, N_BRANCH * D_MODEL)),
        _resident((1, N_BRANCH * D_MODEL)),
    ]


def _proj_conv(x, g1, wq, wkv, wua, wub, wg, bg, cw, cb, lg, lb, seq):
    t = x.shape[0]
    tiles = t // PROJ_T
    row = lambda w: pl.BlockSpec((PROJ_T, w), lambda i: (i, 0))
    return pl.pallas_call(
        functools.partial(_proj_conv_kernel, tiles_per_seq=seq // PROJ_T),
        grid=(tiles,),
        in_specs=_proj_specs() + [_resident((CONV_W, D_CONV)), _resident((1, D_CONV)),
                                  _resident((1, D_CONV)), _resident((1, D_CONV))],
        out_specs=[row(ATT_W), row(2 * KV_W), row(N_BRANCH * D_MODEL), row(D_CONV),
                   pl.BlockSpec((CONV_HALO, D_CONV), lambda i: (i, 0))],
        out_shape=[
            jax.ShapeDtypeStruct((t, ATT_W), BF16),
            jax.ShapeDtypeStruct((t, 2 * KV_W), F32),
            jax.ShapeDtypeStruct((t, N_BRANCH * D_MODEL), BF16),
            jax.ShapeDtypeStruct((t, D_CONV), BF16),
            jax.ShapeDtypeStruct((tiles * CONV_HALO, D_CONV), F32),
        ],
        scratch_shapes=[pltpu.VMEM((D_CONV // LANES, CONV_HALO + PROJ_T, LANES), F32),
                        pltpu.VMEM((PROJ_T, D_CONV), F32)],
        compiler_params=_params(("arbitrary",)),
        name="proj_conv",
    )(x, g1, wq, wkv, wua, wub, wg, bg, cw, cb, lg, lb)


def _proj(x, g1, wq, wkv, wua, wub, wg, bg):
    t = x.shape[0]
    row = lambda w: pl.BlockSpec((PROJ_T, w), lambda i: (i, 0))
    return pl.pallas_call(
        _proj_kernel,
        grid=(t // PROJ_T,),
        in_specs=_proj_specs(),
        out_specs=[row(ATT_W), row(2 * KV_W), row(D_CONV), row(N_BRANCH * D_MODEL)],
        out_shape=[
            jax.ShapeDtypeStruct((t, ATT_W), BF16),
            jax.ShapeDtypeStruct((t, 2 * KV_W), F32),
            jax.ShapeDtypeStruct((t, D_CONV), F32),
            jax.ShapeDtypeStruct((t, N_BRANCH * D_MODEL), BF16),
        ],
        compiler_params=_params(("arbitrary",)),
        name="proj",
    )(x, g1, wq, wkv, wua, wub, wg, bg)


def _sink_softmax(s, valid, sink):
    s = jnp.where(valid, s, NEG)
    m = jnp.maximum(jnp.max(s, axis=-1, keepdims=True), sink)
    e = jnp.exp(s - m)
    den = jnp.sum(e, axis=-1, keepdims=True) + jnp.exp(sink - m)
    return e, 1.0 / den


ATT_Q = 512
ATT_SUB = ATT_Q // WINDOW


def _prompt_attn_kernel(sink_ref, q_ref, kvc_ref, kvp_ref, o_ref, k_buf, v_buf):
    first_tile = pl.program_id(1) == 0
    k_buf[0:WINDOW, :] = kvp_ref[:, 0:KV_W].astype(BF16)
    v_buf[0:WINDOW, :] = kvp_ref[:, KV_W:2 * KV_W].astype(BF16)
    k_buf[WINDOW:WINDOW + ATT_Q, :] = kvc_ref[:, 0:KV_W].astype(BF16)
    v_buf[WINDOW:WINDOW + ATT_Q, :] = kvc_ref[:, KV_W:2 * KV_W].astype(BF16)

    qi = lax.broadcasted_iota(jnp.int32, (WINDOW, 2 * WINDOW), 0)
    kj = lax.broadcasted_iota(jnp.int32, (WINDOW, 2 * WINDOW), 1)
    band = (kj > qi) & (kj <= qi + WINDOW)
    head = lax.broadcasted_iota(jnp.int32, (N_HEADS, 1, 1), 0)
    sink = jnp.zeros((N_HEADS, 1, 1), F32)
    for hd in range(N_HEADS):
        sink = jnp.where(head == hd, sink_ref[hd], sink)

    def sub_block(sb, carry):
        row0 = pl.multiple_of(sb * WINDOW, WINDOW)
        lo = jnp.where(first_tile & (sb == 0), WINDOW, 0)
        valid = band & (kj >= lo)
        scores = []
        for h in range(N_KV_HEADS):
            k_h = k_buf[pl.ds(row0, 2 * WINDOW), h * HEAD_DIM:(h + 1) * HEAD_DIM]
            q_h = jnp.concatenate(
                [q_ref[pl.ds(row0, WINDOW), (h * GROUP + g) * HEAD_DIM:(h * GROUP + g + 1) * HEAD_DIM]
                 for g in range(GROUP)], axis=0)
            scores.append(_dot_nt(q_h, k_h).reshape(GROUP, WINDOW, 2 * WINDOW))
        s = jnp.concatenate(scores, axis=0)
        e, inv = _sink_softmax(s, valid[None], sink)
        e = e.astype(BF16)
        outs = []
        for h in range(N_KV_HEADS):
            v_h = v_buf[pl.ds(row0, 2 * WINDOW), h * HEAD_DIM:(h + 1) * HEAD_DIM]
            e_h = e[h * GROUP:(h + 1) * GROUP].reshape(GROUP * WINDOW, 2 * WINDOW)
            o_h = _dot(e_h, v_h) * inv[h * GROUP:(h + 1) * GROUP].reshape(GROUP * WINDOW, 1)
            outs.extend(o_h[g * WINDOW:(g + 1) * WINDOW, :] for g in range(GROUP))
        o_ref[pl.ds(row0, WINDOW), :] = jnp.concatenate(outs, axis=-1).astype(BF16)
        return carry

    lax.fori_loop(0, ATT_SUB, sub_block, 0)


def _prompt_attn(sink, q, kv, batch, seq):
    tiles = seq // ATT_Q
    prev_idx = lambda b, i: (jnp.maximum((b * tiles + i) * ATT_SUB - 1, 0), 0)
    return pl.pallas_call(
        _prompt_attn_kernel,
        grid=(batch, tiles),
        in_specs=[
            pl.BlockSpec(memory_space=pltpu.SMEM),
            pl.BlockSpec((ATT_Q, ATT_W), lambda b, i: (b * tiles + i, 0)),
            pl.BlockSpec((ATT_Q, 2 * KV_W), lambda b, i: (b * tiles + i, 0)),
            pl.BlockSpec((WINDOW, 2 * KV_W), prev_idx),
        ],
        out_specs=pl.BlockSpec((ATT_Q, ATT_W), lambda b, i: (b * tiles + i, 0)),
        out_shape=jax.ShapeDtypeStruct((batch * seq, ATT_W), BF16),
        scratch_shapes=[pltpu.VMEM((WINDOW + ATT_Q, KV_W), BF16),
                        pltpu.VMEM((WINDOW + ATT_Q, KV_W), BF16)],
        compiler_params=_params(("arbitrary", "arbitrary")),
        name="prompt_attn",
    )(sink, q, kv, kv)


DEC_G = 8
KEY_PAD = 8


def _sample_attn_kernel(sink_ref, q_ref, kv_ref, kvt_ref, ck_ref, cv_ref, o_ref, nk_ref, nv_ref,
                        *, dec_seq):
    rows = GROUP * dec_seq
    n_chain = DEC_G * N_KV_HEADS
    pad = jnp.zeros((KEY_PAD - dec_seq, KV_W), F32)

    scores, values = [], []
    for b in range(DEC_G):
        k_win, v_win = ck_ref[b], cv_ref[b]
        nk_ref[b] = jnp.concatenate([k_win[:, dec_seq:], kvt_ref[b, 0:KV_W, :]], axis=1)
        nv_ref[b] = jnp.concatenate([v_win[:, dec_seq:], kvt_ref[b, KV_W:2 * KV_W, :]], axis=1)
        k_win, v_win = k_win.astype(BF16), v_win.astype(BF16)
        k_new = jnp.concatenate([kv_ref[b, :, 0:KV_W], pad], axis=0).astype(BF16)
        v_new = jnp.concatenate([kv_ref[b, :, KV_W:2 * KV_W], pad], axis=0).astype(BF16)
        for h in range(N_KV_HEADS):
            ch = slice(h * HEAD_DIM, (h + 1) * HEAD_DIM)
            q_h = q_ref[b, h]
            scores.append(jnp.concatenate(
                [_dot(q_h, k_win[ch, :]), _dot_nt(q_h, k_new[:, ch])], axis=1))
            values.append((v_win[ch, :], v_new[:, ch]))

    s = jnp.concatenate(scores, axis=0)
    r = lax.broadcasted_iota(jnp.int32, s.shape, 0)
    c = lax.broadcasted_iota(jnp.int32, s.shape, 1)
    t = r % dec_seq
    valid = ((c < WINDOW) & (c > t)) | ((c >= WINDOW) & (c - WINDOW <= t))
    head = (lax.broadcasted_iota(jnp.int32, (s.shape[0], 1), 0) // dec_seq) % N_HEADS
    sink_col = jnp.zeros((s.shape[0], 1), F32)
    for hd in range(N_HEADS):
        sink_col = jnp.where(head == hd, sink_ref[hd], sink_col)
    e, inv = _sink_softmax(s, valid, sink_col)
    e = e.astype(BF16)

    for i in range(n_chain):
        sl = slice(i * rows, (i + 1) * rows)
        v_win, v_new = values[i]
        o = _dot_nt(e[sl, 0:WINDOW], v_win) + _dot(e[sl, WINDOW:], v_new)
        o_ref[i // N_KV_HEADS, i % N_KV_HEADS] = o * inv[sl, :]


def _sample_attn(sink, q, kv, ck_t, cv_t):
    nb, dec_seq, _ = kv.shape
    rows = GROUP * dec_seq
    blk3 = lambda d1, d2: pl.BlockSpec((DEC_G, d1, d2), lambda i: (i, 0, 0))
    blk4 = pl.BlockSpec((DEC_G, N_KV_HEADS, rows, HEAD_DIM), lambda i: (i, 0, 0, 0))
    return pl.pallas_call(
        functools.partial(_sample_attn_kernel, dec_seq=dec_seq),
        grid=(nb // DEC_G,),
        in_specs=[
            pl.BlockSpec(memory_space=pltpu.SMEM),
            blk4, blk3(dec_seq, 2 * KV_W), blk3(2 * KV_W, dec_seq),
            blk3(KV_W, WINDOW), blk3(KV_W, WINDOW),
        ],
        out_specs=[blk4, blk3(KV_W, WINDOW), blk3(KV_W, WINDOW)],
        out_shape=[
            jax.ShapeDtypeStruct((nb, N_KV_HEADS, rows, HEAD_DIM), F32),
            jax.ShapeDtypeStruct((nb, KV_W, WINDOW), F32),
            jax.ShapeDtypeStruct((nb, KV_W, WINDOW), F32),
        ],
        compiler_params=_params(("arbitrary",)),
        name="sample_attn",
    )(sink, q, kv, jnp.transpose(kv, (0, 2, 1)), ck_t, cv_t)


def _to_head_major(q, nb, dec_seq):
    q = q.reshape(nb, dec_seq, N_KV_HEADS, GROUP, HEAD_DIM).transpose(0, 2, 3, 1, 4)
    return q.reshape(nb, N_KV_HEADS, GROUP * dec_seq, HEAD_DIM)


def _from_head_major(o, nb, dec_seq):
    o = o.reshape(nb, N_KV_HEADS, GROUP, dec_seq, HEAD_DIM).transpose(0, 3, 1, 2, 4)
    return o.reshape(nb * dec_seq, ATT_W)


SUBLANES = 8
CONV_G = 32


def _sample_conv_kernel(st_ref, u_ref, w_ref, cb_ref, g_ref, b_ref, o_ref, ns_ref, *, dec_seq):
    hist = CONV_W - 1
    ns_ref[0:hist - dec_seq] = st_ref[dec_seq:hist]
    ns_ref[hist - dec_seq:hist] = u_ref[...]
    for s0 in range(0, CONV_G, SUBLANES):
        seqs = slice(s0, s0 + SUBLANES)
        accs = [jnp.zeros((SUBLANES, D_CONV), F32) for _ in range(dec_seq)]
        for r in range(hist + dec_seq):
            row = st_ref[r, seqs, :] if r < hist else u_ref[r - hist, seqs, :]
            for t in range(dec_seq):
                j = r - t
                if 0 <= j < CONV_W:
                    accs[t] = accs[t] + w_ref[j:j + 1, :] * row
        for t in range(dec_seq):
            o_ref[t, seqs, :] = _ln_swish(accs[t] + cb_ref[...], g_ref[...], b_ref[...])


def _sample_conv(state_t, u_t, w, cb, g, b):
    dec_seq, nb, _ = u_t.shape
    hist = CONV_W - 1
    blk3 = lambda d0: pl.BlockSpec((d0, CONV_G, D_CONV), lambda i: (0, i, 0))
    return pl.pallas_call(
        functools.partial(_sample_conv_kernel, dec_seq=dec_seq),
        grid=(nb // CONV_G,),
        in_specs=[blk3(hist), blk3(dec_seq), _resident((CONV_W, D_CONV)),
                  _resident((1, D_CONV)), _resident((1, D_CONV)), _resident((1, D_CONV))],
        out_specs=[blk3(dec_seq), blk3(hist)],
        out_shape=[jax.ShapeDtypeStruct((dec_seq, nb, D_CONV), F32),
                   jax.ShapeDtypeStruct((hist, nb, D_CONV), F32)],
        compiler_params=_params(("arbitrary",)),
        name="sample_conv",
    )(state_t, u_t, w, cb, g, b)


MIX_T = 256


def _mix_kernel(att_ref, ycp_ref, gate_ref, x_ref, wa_ref, wc_ref, wo_ref, g2_ref, h_ref, hn_ref):
    ya = _dot(att_ref[...], wa_ref[...])
    yc = _dot(ycp_ref[...], wc_ref[...])
    mixed = (gate_ref[:, 0:D_MODEL].astype(F32) * ya
             + gate_ref[:, D_MODEL:2 * D_MODEL].astype(F32) * yc)
    h = x_ref[...] + _dot(mixed.astype(BF16), wo_ref[...])
    h_ref[...] = h
    hn_ref[...] = _rms(h, g2_ref[...]).astype(BF16)


def _mix(att, ycp, gates, x, wa, wc, wo, g2):
    t = x.shape[0]
    row = lambda w: pl.BlockSpec((MIX_T, w), lambda i: (i, 0))
    return pl.pallas_call(
        _mix_kernel,
        grid=(t // MIX_T,),
        in_specs=[row(ATT_W), row(D_CONV), row(N_BRANCH * D_MODEL), row(D_MODEL),
                  _resident((ATT_W, D_MODEL)), _resident((D_CONV, D_MODEL)),
                  _resident((D_MODEL, D_MODEL)), _resident((1, D_MODEL))],
        out_specs=[row(D_MODEL), row(D_MODEL)],
        out_shape=[jax.ShapeDtypeStruct((t, D_MODEL), F32),
                   jax.ShapeDtypeStruct((t, D_MODEL), BF16)],
        compiler_params=_params(("arbitrary",)),
        name="mix",
    )(att, ycp, gates, x, wa, wc, wo, g2)


MLP_T = 512
MLP_F = 1024


def _mlp_kernel(hn_ref, h_ref, wu_ref, wd_ref, gf_ref, y_ref, *, final_norm):
    f = pl.program_id(1)

    @pl.when(f == 0)
    def _():
        y_ref[...] = h_ref[...]

    z = jnp.maximum(_dot(hn_ref[...], wu_ref[...]), 0.0)
    y_ref[...] += _dot((z * z).astype(BF16), wd_ref[...])

    if final_norm:
        @pl.when(f == pl.num_programs(1) - 1)
        def _():
            y_ref[...] = _rms(y_ref[...], gf_ref[...])


def _mlp(hn, h, wu, wd, gf, final_norm):
    t = h.shape[0]
    return pl.pallas_call(
        functools.partial(_mlp_kernel, final_norm=final_norm),
        grid=(t // MLP_T, D_FF // MLP_F),
        in_specs=[
            pl.BlockSpec((MLP_T, D_MODEL), lambda i, f: (i, 0)),
            pl.BlockSpec((MLP_T, D_MODEL), lambda i, f: (i, 0)),
            pl.BlockSpec((D_MODEL, MLP_F), lambda i, f: (0, f)),
            pl.BlockSpec((MLP_F, D_MODEL), lambda i, f: (f, 0)),
            _resident((1, D_MODEL)),
        ],
        out_specs=pl.BlockSpec((MLP_T, D_MODEL), lambda i, f: (i, 0)),
        out_shape=jax.ShapeDtypeStruct((t, D_MODEL), F32),
        compiler_params=_params(("arbitrary", "arbitrary")),
        name="mlp",
    )(hn, h, wu, wd, gf)


def _split_w_in(w_in, b_gate):
    o_k = ATT_W
    o_u = ATT_W + 2 * KV_W
    o_g = o_u + 2 * D_CONV
    cols = ((0, o_k), (o_k, o_u), (o_u, o_u + D_CONV), (o_u + D_CONV, o_g), (o_g, w_in.shape[1]))
    return (*[w_in[:, a:b].astype(BF16) for a, b in cols], b_gate.reshape(1, -1))


def kernel(x_prompt, x_sample, cache_k, cache_v, state_conv, norm1_g, w_in, b_gate, sink,
           w_attn_o, conv_w, conv_b, cln_g, cln_b, w_conv_o, w_out, norm2_g, w_up, w_down,
           norm_f_g):
    depth = w_in.shape[0]
    batch, seq, _ = x_prompt.shape
    nb, dec_seq, _ = x_sample.shape
    hist = CONV_W - 1
    hp = x_prompt.reshape(batch * seq, D_MODEL)
    hs = x_sample.reshape(nb * dec_seq, D_MODEL)
    row = lambda v: v.reshape(1, -1)
    gf = row(norm_f_g)
    outs = [[] for _ in range(6)]
    for l in range(depth):
        last = l == depth - 1
        wq, wkv, wua, wub, wg, bg = _split_w_in(w_in[l], b_gate[l])
        wa, wc, wo = w_attn_o[l].astype(BF16), w_conv_o[l].astype(BF16), w_out[l].astype(BF16)
        wu, wd = w_up[l].astype(BF16), w_down[l].astype(BF16)
        g1, g2 = row(norm1_g[l]), row(norm2_g[l])
        cb, lg, lb = row(conv_b[l]), row(cln_g[l]), row(cln_b[l])

        q, kv, gates, ycp, u_tail = _proj_conv(hp, g1, wq, wkv, wua, wub, wg, bg,
                                               conv_w[l], cb, lg, lb, seq)
        att = _prompt_attn(sink[l], q, kv, batch, seq)
        h, hn = _mix(att, ycp, gates, hp, wa, wc, wo, g2)
        hp = _mlp(hn, h, wu, wd, gf, final_norm=last)
        kv3 = kv.reshape(batch, seq, 2 * KV_W)
        outs[0].append(kv3[:, seq - WINDOW:, :KV_W].reshape(batch, WINDOW, N_KV_HEADS, HEAD_DIM))
        outs[1].append(kv3[:, seq - WINDOW:, KV_W:].reshape(batch, WINDOW, N_KV_HEADS, HEAD_DIM))
        u_tail = u_tail.reshape(batch, seq // PROJ_T, CONV_HALO, D_CONV)
        outs[2].append(u_tail[:, -1, CONV_HALO - hist:, :])

        q, kv, u, gates = _proj(hs, g1, wq, wkv, wua, wub, wg, bg)
        att, nk, nv = _sample_attn(
            sink[l], _to_head_major(q, nb, dec_seq), kv.reshape(nb, dec_seq, 2 * KV_W),
            jnp.transpose(cache_k[l].reshape(nb, WINDOW, KV_W), (0, 2, 1)),
            jnp.transpose(cache_v[l].reshape(nb, WINDOW, KV_W), (0, 2, 1)))
        ycp_t, ns_t = _sample_conv(
            jnp.transpose(state_conv[l], (1, 0, 2)),
            jnp.transpose(u.reshape(nb, dec_seq, D_CONV), (1, 0, 2)), conv_w[l], cb, lg, lb)
        ycp = jnp.transpose(ycp_t, (1, 0, 2)).reshape(nb * dec_seq, D_CONV)
        h, hn = _mix(_from_head_major(att, nb, dec_seq).astype(BF16),
                     ycp.astype(BF16), gates, hs, wa, wc, wo, g2)
        hs = _mlp(hn, h, wu, wd, gf, final_norm=last)
        outs[3].append(jnp.transpose(nk, (0, 2, 1)).reshape(nb, WINDOW, N_KV_HEADS, HEAD_DIM))
        outs[4].append(jnp.transpose(nv, (0, 2, 1)).reshape(nb, WINDOW, N_KV_HEADS, HEAD_DIM))
        outs[5].append(jnp.transpose(ns_t, (1, 0, 2)))

    return (hp.reshape(batch, seq, D_MODEL), hs.reshape(nb, dec_seq, D_MODEL),
            *[jnp.stack(o) for o in outs])
```

```python
import functools

import jax
import jax.numpy as jnp
from jax import lax
from jax.experimental import pallas as pl
from jax.experimental.pallas import tpu as pltpu

D_MODEL = 2048
HEAD_DIM = 64
N_HEADS = 16
N_KV_HEADS = 4
GROUP = N_HEADS // N_KV_HEADS
ATT_W = N_HEADS * HEAD_DIM
KV_W = N_KV_HEADS * HEAD_DIM
WINDOW = 128
D_CONV = D_MODEL // 2
CONV_W = 31
N_BRANCH = 2
D_FF = 4 * D_MODEL
EPS = 1e-6
NEG = -1e30
ATTN_SCALE = HEAD_DIM ** -0.5

F32 = jnp.float32
BF16 = jnp.bfloat16

V7X_VMEM_LIMIT_BYTES = 58 * 1024 * 1024


def _dot(a, b):
    return jnp.dot(a, b, preferred_element_type=F32)


def _dot_nt(a, b):
    return lax.dot_general(a, b, (((1,), (1,)), ((), ())), preferred_element_type=F32)


def _sigmoid(x):
    return 1.0 / (1.0 + jnp.exp(-x))


def _rms(x, g):
    return x * lax.rsqrt(jnp.mean(x * x, axis=-1, keepdims=True) + EPS) * g


def _params(sem):
    return pltpu.CompilerParams(dimension_semantics=sem, vmem_limit_bytes=V7X_VMEM_LIMIT_BYTES)


def _resident(shape):
    return pl.BlockSpec(shape, lambda *_: (0,) * len(shape), pipeline_mode=pl.Buffered(1))


PROJ_T = 256
CONV_HALO = 32
CONV_ROWS = 64
LANES = 128


def _ln_swish(c, g, b):
    mu = jnp.mean(c, axis=-1, keepdims=True)
    xc = c - mu
    y = xc * lax.rsqrt(jnp.mean(xc * xc, axis=-1, keepdims=True) + EPS) * g + b
    return y * _sigmoid(y)


COL_K = ATT_W
COL_U = ATT_W + 2 * KV_W
COL_UB = COL_U + D_CONV
COL_G = COL_U + 2 * D_CONV
IN_W = COL_G + N_BRANCH * D_MODEL


def _project(x_ref, g1_ref, w_ref, bg_ref):
    xn = _rms(x_ref[...], g1_ref[...]).astype(BF16)
    q = (_dot(xn, w_ref[:, 0:COL_K]) * ATTN_SCALE).astype(BF16)
    kv = _dot(xn, w_ref[:, COL_K:COL_U])
    u = _dot(xn, w_ref[:, COL_U:COL_UB]) * _sigmoid(_dot(xn, w_ref[:, COL_UB:COL_G]))
    gates = _sigmoid(_dot(xn, w_ref[:, COL_G:IN_W]) + bg_ref[...]).astype(BF16)
    return q, kv, u, gates


def _proj_kernel(x_ref, g1_ref, w_ref, bg_ref, q_ref, kv_ref, u_ref, gate_ref):
    q_ref[...], kv_ref[...], u_ref[...], gate_ref[...] = _project(x_ref, g1_ref, w_ref, bg_ref)


def _proj_conv_kernel(x_ref, g1_ref, w_ref, bg_ref, cw_ref, cb_ref, lg_ref, lb_ref,
                      q_ref, kv_ref, gate_ref, ycp_ref, tail_ref, buf, acc, *, tiles_per_seq):
    first = pl.program_id(0) % tiles_per_seq == 0

    @pl.when(first)
    def _():
        buf[:, 0:CONV_HALO, :] = jnp.zeros((D_CONV // LANES, CONV_HALO, LANES), F32)

    @pl.when(jnp.logical_not(first))
    def _():
        buf[:, 0:CONV_HALO, :] = buf[:, PROJ_T:PROJ_T + CONV_HALO, :]

    q_ref[...], kv_ref[...], u, gate_ref[...] = _project(x_ref, g1_ref, w_ref, bg_ref)
    tail_ref[...] = u[PROJ_T - CONV_HALO:, :]
    for c in range(D_CONV // LANES):
        buf[c, CONV_HALO:CONV_HALO + PROJ_T, :] = u[:, c * LANES:(c + 1) * LANES]
    off = CONV_HALO - (CONV_W - 1)
    for c in range(D_CONV // LANES):
        lanes = slice(c * LANES, (c + 1) * LANES)
        for r0 in range(0, PROJ_T, CONV_ROWS):
            a = jnp.zeros((CONV_ROWS, LANES), F32)
            for j in range(CONV_W):
                a = a + cw_ref[j:j + 1, lanes] * buf[c, r0 + off + j:r0 + off + j + CONV_ROWS, :]
            acc[r0:r0 + CONV_ROWS, lanes] = a
    ycp_ref[...] = _ln_swish(acc[...] + cb_ref[...], lg_ref[...], lb_ref[...]).astype(BF16)


def _proj_specs():
    return [
        pl.BlockSpec((PROJ_T, D_MODEL), lambda i: (i, 0)),
        _resident((1, D_MODEL)),
        _resident((D_MODEL, IN_W)),
        _resident((1, N_BRANCH * D_MODEL)),
    ]


def _proj_conv(x, g1, w, bg, cw, cb, lg, lb, seq):
    t = x.shape[0]
    tiles = t // PROJ_T
    row = lambda w: pl.BlockSpec((PROJ_T, w), lambda i: (i, 0))
    return pl.pallas_call(
        functools.partial(_proj_conv_kernel, tiles_per_seq=seq // PROJ_T),
        grid=(tiles,),
        in_specs=_proj_specs() + [_resident((CONV_W, D_CONV)), _resident((1, D_CONV)),
                                  _resident((1, D_CONV)), _resident((1, D_CONV))],
        out_specs=[row(ATT_W), row(2 * KV_W), row(N_BRANCH * D_MODEL), row(D_CONV),
                   pl.BlockSpec((CONV_HALO, D_CONV), lambda i: (i, 0))],
        out_shape=[
            jax.ShapeDtypeStruct((t, ATT_W), BF16),
            jax.ShapeDtypeStruct((t, 2 * KV_W), F32),
            jax.ShapeDtypeStruct((t, N_BRANCH * D_MODEL), BF16),
            jax.ShapeDtypeStruct((t, D_CONV), BF16),
            jax.ShapeDtypeStruct((tiles * CONV_HALO, D_CONV), F32),
        ],
        scratch_shapes=[pltpu.VMEM((D_CONV // LANES, CONV_HALO + PROJ_T, LANES), F32),
                        pltpu.VMEM((PROJ_T, D_CONV), F32)],
        compiler_params=_params(("arbitrary",)),
        name="proj_conv",
    )(x, g1, w, bg, cw, cb, lg, lb)


def _proj(x, g1, w, bg):
    t = x.shape[0]
    row = lambda w: pl.BlockSpec((PROJ_T, w), lambda i: (i, 0))
    return pl.pallas_call(
        _proj_kernel,
        grid=(t // PROJ_T,),
        in_specs=_proj_specs(),
        out_specs=[row(ATT_W), row(2 * KV_W), row(D_CONV), row(N_BRANCH * D_MODEL)],
        out_shape=[
            jax.ShapeDtypeStruct((t, ATT_W), BF16),
            jax.ShapeDtypeStruct((t, 2 * KV_W), F32),
            jax.ShapeDtypeStruct((t, D_CONV), F32),
            jax.ShapeDtypeStruct((t, N_BRANCH * D_MODEL), BF16),
        ],
        compiler_params=_params(("arbitrary",)),
        name="proj",
    )(x, g1, w, bg)


def _sink_softmax(s, valid, sink):
    s = jnp.where(valid, s, NEG)
    m = jnp.maximum(jnp.max(s, axis=-1, keepdims=True), sink)
    e = jnp.exp(s - m)
    den = jnp.sum(e, axis=-1, keepdims=True) + jnp.exp(sink - m)
    return e, 1.0 / den


ATT_Q = 512
ATT_SUB = ATT_Q // WINDOW


def _prompt_attn_kernel(sink_ref, q_ref, kvc_ref, kvp_ref, o_ref, k_buf, v_buf):
    first_tile = pl.program_id(1) == 0
    k_buf[0:WINDOW, :] = kvp_ref[:, 0:KV_W].astype(BF16)
    v_buf[0:WINDOW, :] = kvp_ref[:, KV_W:2 * KV_W].astype(BF16)
    k_buf[WINDOW:WINDOW + ATT_Q, :] = kvc_ref[:, 0:KV_W].astype(BF16)
    v_buf[WINDOW:WINDOW + ATT_Q, :] = kvc_ref[:, KV_W:2 * KV_W].astype(BF16)

    qi = lax.broadcasted_iota(jnp.int32, (WINDOW, 2 * WINDOW), 0)
    kj = lax.broadcasted_iota(jnp.int32, (WINDOW, 2 * WINDOW), 1)
    band = (kj > qi) & (kj <= qi + WINDOW)
    head = lax.broadcasted_iota(jnp.int32, (N_HEADS, 1, 1), 0)
    sink = jnp.zeros((N_HEADS, 1, 1), F32)
    for hd in range(N_HEADS):
        sink = jnp.where(head == hd, sink_ref[hd], sink)

    def sub_block(sb, carry):
        row0 = pl.multiple_of(sb * WINDOW, WINDOW)
        lo = jnp.where(first_tile & (sb == 0), WINDOW, 0)
        valid = band & (kj >= lo)
        scores = []
        for h in range(N_KV_HEADS):
            k_h = k_buf[pl.ds(row0, 2 * WINDOW), h * HEAD_DIM:(h + 1) * HEAD_DIM]
            q_h = jnp.concatenate(
                [q_ref[pl.ds(row0, WINDOW), (h * GROUP + g) * HEAD_DIM:(h * GROUP + g + 1) * HEAD_DIM]
                 for g in range(GROUP)], axis=0)
            scores.append(_dot_nt(q_h, k_h).reshape(GROUP, WINDOW, 2 * WINDOW))
        s = jnp.concatenate(scores, axis=0)
        e, inv = _sink_softmax(s, valid[None], sink)
        e = e.astype(BF16)
        outs = []
        for h in range(N_KV_HEADS):
            v_h = v_buf[pl.ds(row0, 2 * WINDOW), h * HEAD_DIM:(h + 1) * HEAD_DIM]
            e_h = e[h * GROUP:(h + 1) * GROUP].reshape(GROUP * WINDOW, 2 * WINDOW)
            o_h = _dot(e_h, v_h) * inv[h * GROUP:(h + 1) * GROUP].reshape(GROUP * WINDOW, 1)
            outs.extend(o_h[g * WINDOW:(g + 1) * WINDOW, :] for g in range(GROUP))
        o_ref[pl.ds(row0, WINDOW), :] = jnp.concatenate(outs, axis=-1).astype(BF16)
        return carry

    lax.fori_loop(0, ATT_SUB, sub_block, 0)


def _prompt_attn(sink, q, kv, batch, seq):
    tiles = seq // ATT_Q
    prev_idx = lambda b, i: (jnp.maximum((b * tiles + i) * ATT_SUB - 1, 0), 0)
    return pl.pallas_call(
        _prompt_attn_kernel,
        grid=(batch, tiles),
        in_specs=[
            pl.BlockSpec(memory_space=pltpu.SMEM),
            pl.BlockSpec((ATT_Q, ATT_W), lambda b, i: (b * tiles + i, 0)),
            pl.BlockSpec((ATT_Q, 2 * KV_W), lambda b, i: (b * tiles + i, 0)),
            pl.BlockSpec((WINDOW, 2 * KV_W), prev_idx),
        ],
        out_specs=pl.BlockSpec((ATT_Q, ATT_W), lambda b, i: (b * tiles + i, 0)),
        out_shape=jax.ShapeDtypeStruct((batch * seq, ATT_W), BF16),
        scratch_shapes=[pltpu.VMEM((WINDOW + ATT_Q, KV_W), BF16),
                        pltpu.VMEM((WINDOW + ATT_Q, KV_W), BF16)],
        compiler_params=_params(("arbitrary", "arbitrary")),
        name="prompt_attn",
    )(sink, q, kv, kv)


DEC_G = 8
KEY_PAD = 8


def _sample_attn_kernel(sink_ref, q_ref, kv_ref, kvt_ref, ck_ref, cv_ref, o_ref, nk_ref, nv_ref,
                        *, dec_seq):
    rows = GROUP * dec_seq
    n_chain = DEC_G * N_KV_HEADS
    pad = jnp.zeros((KEY_PAD - dec_seq, KV_W), F32)

    scores, values = [], []
    for b in range(DEC_G):
        k_win, v_win = ck_ref[b], cv_ref[b]
        new = slice(b * dec_seq, (b + 1) * dec_seq)
        nk_ref[b] = jnp.concatenate([k_win[:, dec_seq:], kvt_ref[0:KV_W, new]], axis=1)
        nv_ref[b] = jnp.concatenate([v_win[:, dec_seq:], kvt_ref[KV_W:2 * KV_W, new]], axis=1)
        k_win, v_win = k_win.astype(BF16), v_win.astype(BF16)
        k_new = jnp.concatenate([kv_ref[b, :, 0:KV_W], pad], axis=0).astype(BF16)
        v_new = jnp.concatenate([kv_ref[b, :, KV_W:2 * KV_W], pad], axis=0).astype(BF16)
        for h in range(N_KV_HEADS):
            ch = slice(h * HEAD_DIM, (h + 1) * HEAD_DIM)
            q_h = q_ref[b, h]
            scores.append(jnp.concatenate(
                [_dot(q_h, k_win[ch, :]), _dot_nt(q_h, k_new[:, ch])], axis=1))
            values.append((v_win[ch, :], v_new[:, ch]))

    s = jnp.concatenate(scores, axis=0)
    r = lax.broadcasted_iota(jnp.int32, s.shape, 0)
    c = lax.broadcasted_iota(jnp.int32, s.shape, 1)
    t = r % dec_seq
    valid = ((c < WINDOW) & (c > t)) | ((c >= WINDOW) & (c - WINDOW <= t))
    head = (lax.broadcasted_iota(jnp.int32, (s.shape[0], 1), 0) // dec_seq) % N_HEADS
    sink_col = jnp.zeros((s.shape[0], 1), F32)
    for hd in range(N_HEADS):
        sink_col = jnp.where(head == hd, sink_ref[hd], sink_col)
    e, inv = _sink_softmax(s, valid, sink_col)
    e = e.astype(BF16)

    for i in range(n_chain):
        sl = slice(i * rows, (i + 1) * rows)
        v_win, v_new = values[i]
        o = _dot_nt(e[sl, 0:WINDOW], v_win) + _dot(e[sl, WINDOW:], v_new)
        o_ref[i // N_KV_HEADS, i % N_KV_HEADS] = o * inv[sl, :]


def _sample_attn(sink, q, kv, ck_t, cv_t):
    nb, dec_seq, _ = kv.shape
    rows = GROUP * dec_seq
    blk3 = lambda d1, d2: pl.BlockSpec((DEC_G, d1, d2), lambda i: (i, 0, 0))
    blk4 = pl.BlockSpec((DEC_G, N_KV_HEADS, rows, HEAD_DIM), lambda i: (i, 0, 0, 0))
    return pl.pallas_call(
        functools.partial(_sample_attn_kernel, dec_seq=dec_seq),
        grid=(nb // DEC_G,),
        in_specs=[
            pl.BlockSpec(memory_space=pltpu.SMEM),
            blk4, blk3(dec_seq, 2 * KV_W),
            pl.BlockSpec((None, 2 * KV_W, DEC_G * dec_seq), lambda i: (i, 0, 0)),
            blk3(KV_W, WINDOW), blk3(KV_W, WINDOW),
        ],
        out_specs=[blk4, blk3(KV_W, WINDOW), blk3(KV_W, WINDOW)],
        out_shape=[
            jax.ShapeDtypeStruct((nb, N_KV_HEADS, rows, HEAD_DIM), F32),
            jax.ShapeDtypeStruct((nb, KV_W, WINDOW), F32),
            jax.ShapeDtypeStruct((nb, KV_W, WINDOW), F32),
        ],
        compiler_params=_params(("arbitrary",)),
        name="sample_attn",
    )(sink, q, kv,
      jnp.transpose(kv.reshape(nb // DEC_G, DEC_G * dec_seq, 2 * KV_W), (0, 2, 1)), ck_t, cv_t)


def _to_head_major(q, nb, dec_seq):
    q = q.reshape(nb, dec_seq, N_KV_HEADS, GROUP, HEAD_DIM).transpose(0, 2, 3, 1, 4)
    return q.reshape(nb, N_KV_HEADS, GROUP * dec_seq, HEAD_DIM)


def _from_head_major(o, nb, dec_seq):
    o = o.reshape(nb, N_KV_HEADS, GROUP, dec_seq, HEAD_DIM).transpose(0, 3, 1, 2, 4)
    return o.reshape(nb * dec_seq, ATT_W)


SUBLANES = 8
CONV_G = 32


def _sample_conv_kernel(st_ref, u_ref, w_ref, cb_ref, g_ref, b_ref, o_ref, ns_ref, *, dec_seq):
    hist = CONV_W - 1
    ns_ref[0:hist - dec_seq] = st_ref[dec_seq:hist]
    ns_ref[hist - dec_seq:hist] = u_ref[...]
    for s0 in range(0, CONV_G, SUBLANES):
        seqs = slice(s0, s0 + SUBLANES)
        accs = [jnp.zeros((SUBLANES, D_CONV), F32) for _ in range(dec_seq)]
        for r in range(hist + dec_seq):
            row = st_ref[r, seqs, :] if r < hist else u_ref[r - hist, seqs, :]
            for t in range(dec_seq):
                j = r - t
                if 0 <= j < CONV_W:
                    accs[t] = accs[t] + w_ref[j:j + 1, :] * row
        for t in range(dec_seq):
            o_ref[t, seqs, :] = _ln_swish(accs[t] + cb_ref[...], g_ref[...], b_ref[...])


def _sample_conv(state_t, u_t, w, cb, g, b):
    dec_seq, nb, _ = u_t.shape
    hist = CONV_W - 1
    blk3 = lambda d0: pl.BlockSpec((d0, CONV_G, D_CONV), lambda i: (0, i, 0))
    return pl.pallas_call(
        functools.partial(_sample_conv_kernel, dec_seq=dec_seq),
        grid=(nb // CONV_G,),
        in_specs=[blk3(hist), blk3(dec_seq), _resident((CONV_W, D_CONV)),
                  _resident((1, D_CONV)), _resident((1, D_CONV)), _resident((1, D_CONV))],
        out_specs=[blk3(dec_seq), blk3(hist)],
        out_shape=[jax.ShapeDtypeStruct((dec_seq, nb, D_CONV), F32),
                   jax.ShapeDtypeStruct((hist, nb, D_CONV), F32)],
        compiler_params=_params(("arbitrary",)),
        name="sample_conv",
    )(state_t, u_t, w, cb, g, b)


MIX_T = 512
MIX_SUB = 256


def _mix_kernel(att_ref, ycp_ref, gate_ref, x_ref, wa_ref, wc_ref, wo_ref, g2_ref, h_ref, hn_ref):
    for r0 in range(0, MIX_T, MIX_SUB):
        rows = slice(r0, r0 + MIX_SUB)
        ya = _dot(att_ref[rows, :], wa_ref[...])
        yc = _dot(ycp_ref[rows, :], wc_ref[...])
        mixed = (gate_ref[rows, 0:D_MODEL].astype(F32) * ya
                 + gate_ref[rows, D_MODEL:2 * D_MODEL].astype(F32) * yc)
        h = x_ref[rows, :] + _dot(mixed.astype(BF16), wo_ref[...])
        h_ref[rows, :] = h
        hn_ref[rows, :] = _rms(h, g2_ref[...]).astype(BF16)


def _mix(att, ycp, gates, x, wa, wc, wo, g2):
    t = x.shape[0]
    row = lambda w: pl.BlockSpec((MIX_T, w), lambda i: (i, 0))
    return pl.pallas_call(
        _mix_kernel,
        grid=(t // MIX_T,),
        in_specs=[row(ATT_W), row(D_CONV), row(N_BRANCH * D_MODEL), row(D_MODEL),
                  _resident((ATT_W, D_MODEL)), _resident((D_CONV, D_MODEL)),
                  _resident((D_MODEL, D_MODEL)), _resident((1, D_MODEL))],
        out_specs=[row(D_MODEL), row(D_MODEL)],
        out_shape=[jax.ShapeDtypeStruct((t, D_MODEL), F32),
                   jax.ShapeDtypeStruct((t, D_MODEL), BF16)],
        compiler_params=_params(("arbitrary",)),
        name="mix",
    )(att, ycp, gates, x, wa, wc, wo, g2)


MLP_T = 512
MLP_F = 1024


def _mlp_kernel(hn_ref, h_ref, wu_ref, wd_ref, gf_ref, y_ref, *, final_norm):
    f = pl.program_id(1)

    @pl.when(f == 0)
    def _():
        y_ref[...] = h_ref[...]

    z = jnp.maximum(_dot(hn_ref[...], wu_ref[...]), 0.0)
    y_ref[...] += _dot((z * z).astype(BF16), wd_ref[...])

    if final_norm:
        @pl.when(f == pl.num_programs(1) - 1)
        def _():
            y_ref[...] = _rms(y_ref[...], gf_ref[...])


def _mlp(hn, h, wu, wd, gf, final_norm):
    t = h.shape[0]
    return pl.pallas_call(
        functools.partial(_mlp_kernel, final_norm=final_norm),
        grid=(t // MLP_T, D_FF // MLP_F),
        in_specs=[
            pl.BlockSpec((MLP_T, D_MODEL), lambda i, f: (i, 0)),
            pl.BlockSpec((MLP_T, D_MODEL), lambda i, f: (i, 0)),
            pl.BlockSpec((D_MODEL, MLP_F), lambda i, f: (0, f)),
            pl.BlockSpec((MLP_F, D_MODEL), lambda i, f: (f, 0)),
            _resident((1, D_MODEL)),
        ],
        out_specs=pl.BlockSpec((MLP_T, D_MODEL), lambda i, f: (i, 0)),
        out_shape=jax.ShapeDtypeStruct((t, D_MODEL), F32),
        compiler_params=_params(("arbitrary", "arbitrary")),
        name="mlp",
    )(hn, h, wu, wd, gf)


def kernel(x_prompt, x_sample, cache_k, cache_v, state_conv, norm1_g, w_in, b_gate, sink,
           w_attn_o, conv_w, conv_b, cln_g, cln_b, w_conv_o, w_out, norm2_g, w_up, w_down,
           norm_f_g):
    depth = w_in.shape[0]
    batch, seq, _ = x_prompt.shape
    nb, dec_seq, _ = x_sample.shape
    hist = CONV_W - 1
    hp = x_prompt.reshape(batch * seq, D_MODEL)
    hs = x_sample.reshape(nb * dec_seq, D_MODEL)
    row = lambda v: v.reshape(1, -1)
    gf = row(norm_f_g)
    outs = [[] for _ in range(6)]
    for l in range(depth):
        last = l == depth - 1
        wi, bg = w_in[l].astype(BF16), row(b_gate[l])
        wa, wc, wo = w_attn_o[l].astype(BF16), w_conv_o[l].astype(BF16), w_out[l].astype(BF16)
        wu, wd = w_up[l].astype(BF16), w_down[l].astype(BF16)
        g1, g2 = row(norm1_g[l]), row(norm2_g[l])
        cb, lg, lb = row(conv_b[l]), row(cln_g[l]), row(cln_b[l])

        q, kv, gates, ycp, u_tail = _proj_conv(hp, g1, wi, bg, conv_w[l], cb, lg, lb, seq)
        att = _prompt_attn(sink[l], q, kv, batch, seq)
        h, hn = _mix(att, ycp, gates, hp, wa, wc, wo, g2)
        hp = _mlp(hn, h, wu, wd, gf, final_norm=last)
        kv3 = kv.reshape(batch, seq, 2 * KV_W)
        outs[0].append(kv3[:, seq - WINDOW:, :KV_W].reshape(batch, WINDOW, N_KV_HEADS, HEAD_DIM))
        outs[1].append(kv3[:, seq - WINDOW:, KV_W:].reshape(batch, WINDOW, N_KV_HEADS, HEAD_DIM))
        u_tail = u_tail.reshape(batch, seq // PROJ_T, CONV_HALO, D_CONV)
        outs[2].append(u_tail[:, -1, CONV_HALO - hist:, :])

        q, kv, u, gates = _proj(hs, g1, wi, bg)
        att, nk, nv = _sample_attn(
            sink[l], _to_head_major(q, nb, dec_seq), kv.reshape(nb, dec_seq, 2 * KV_W),
            jnp.transpose(cache_k[l].reshape(nb, WINDOW, KV_W), (0, 2, 1)),
            jnp.transpose(cache_v[l].reshape(nb, WINDOW, KV_W), (0, 2, 1)))
        ycp_t, ns_t = _sample_conv(
            jnp.transpose(state_conv[l], (1, 0, 2)),
            jnp.transpose(u.reshape(nb, dec_seq, D_CONV), (1, 0, 2)), conv_w[l], cb, lg, lb)
        ycp = jnp.transpose(ycp_t, (1, 0, 2)).reshape(nb * dec_seq, D_CONV)
        h, hn = _mix(_from_head_major(att, nb, dec_seq).astype(BF16),
                     ycp.astype(BF16), gates, hs, wa, wc, wo, g2)
        hs = _mlp(hn, h, wu, wd, gf, final_norm=last)
        outs[3].append(jnp.transpose(nk, (0, 2, 1)).reshape(nb, WINDOW, N_KV_HEADS, HEAD_DIM))
        outs[4].append(jnp.transpose(nv, (0, 2, 1)).reshape(nb, WINDOW, N_KV_HEADS, HEAD_DIM))
        outs[5].append(jnp.transpose(ns_t, (1, 0, 2)))

    return (hp.reshape(batch, seq, D_MODEL), hs.reshape(nb, dec_seq, D_MODEL),
            *[jnp.stack(o) for o in outs])
```

```python
import functools

import jax
import jax.numpy as jnp
from jax import lax
from jax.experimental import pallas as pl
from jax.experimental.pallas import tpu as pltpu

D_MODEL = 2048
HEAD_DIM = 64
N_HEADS = 16
N_KV_HEADS = 4
GROUP = N_HEADS // N_KV_HEADS
ATT_W = N_HEADS * HEAD_DIM
KV_W = N_KV_HEADS * HEAD_DIM
WINDOW = 128
D_CONV = D_MODEL // 2
CONV_W = 31
N_BRANCH = 2
D_FF = 4 * D_MODEL
EPS = 1e-6
NEG = -1e30
ATTN_SCALE = HEAD_DIM ** -0.5

F32 = jnp.float32
BF16 = jnp.bfloat16

V7X_VMEM_LIMIT_BYTES = 58 * 1024 * 1024


def _dot(a, b):
    return jnp.dot(a, b, preferred_element_type=F32)


def _dot_nt(a, b):
    return lax.dot_general(a, b, (((1,), (1,)), ((), ())), preferred_element_type=F32)


def _sigmoid(x):
    return 1.0 / (1.0 + jnp.exp(-x))


def _rms(x, g):
    return x * lax.rsqrt(jnp.mean(x * x, axis=-1, keepdims=True) + EPS) * g


def _params(sem):
    return pltpu.CompilerParams(dimension_semantics=sem, vmem_limit_bytes=V7X_VMEM_LIMIT_BYTES)


def _resident(shape):
    return pl.BlockSpec(shape, lambda *_: (0,) * len(shape), pipeline_mode=pl.Buffered(1))


PROJ_T = 256
CONV_HALO = 32
CONV_ROWS = 64
LANES = 128


def _ln_swish(c, g, b):
    mu = jnp.mean(c, axis=-1, keepdims=True)
    xc = c - mu
    y = xc * lax.rsqrt(jnp.mean(xc * xc, axis=-1, keepdims=True) + EPS) * g + b
    return y * _sigmoid(y)


COL_K = ATT_W
COL_U = ATT_W + 2 * KV_W
COL_UB = COL_U + D_CONV
COL_G = COL_U + 2 * D_CONV
IN_W = COL_G + N_BRANCH * D_MODEL


def _project(x_ref, g1_ref, w_ref, bg_ref):
    xn = _rms(x_ref[...], g1_ref[...]).astype(BF16)
    q = (_dot(xn, w_ref[:, 0:COL_K]) * ATTN_SCALE).astype(BF16)
    kv = _dot(xn, w_ref[:, COL_K:COL_U])
    u = _dot(xn, w_ref[:, COL_U:COL_UB]) * _sigmoid(_dot(xn, w_ref[:, COL_UB:COL_G]))
    gates = _sigmoid(_dot(xn, w_ref[:, COL_G:IN_W]) + bg_ref[...]).astype(BF16)
    return q, kv, u, gates


def _proj_kernel(x_ref, g1_ref, w_ref, bg_ref, q_ref, kv_ref, u_ref, gate_ref):
    q_ref[...], kv_ref[...], u_ref[...], gate_ref[...] = _project(x_ref, g1_ref, w_ref, bg_ref)


def _proj_conv_kernel(x_ref, g1_ref, w_ref, bg_ref, cw_ref, cb_ref, lg_ref, lb_ref,
                      q_ref, kv_ref, gate_ref, ycp_ref, tail_ref, buf, acc, *, tiles_per_seq):
    first = pl.program_id(0) % tiles_per_seq == 0

    @pl.when(first)
    def _():
        buf[:, 0:CONV_HALO, :] = jnp.zeros((D_CONV // LANES, CONV_HALO, LANES), F32)

    @pl.when(jnp.logical_not(first))
    def _():
        buf[:, 0:CONV_HALO, :] = buf[:, PROJ_T:PROJ_T + CONV_HALO, :]

    q_ref[...], kv_ref[...], u, gate_ref[...] = _project(x_ref, g1_ref, w_ref, bg_ref)
    tail_ref[...] = u[PROJ_T - CONV_HALO:, :]
    for c in range(D_CONV // LANES):
        buf[c, CONV_HALO:CONV_HALO + PROJ_T, :] = u[:, c * LANES:(c + 1) * LANES]
    off = CONV_HALO - (CONV_W - 1)
    for c in range(D_CONV // LANES):
        lanes = slice(c * LANES, (c + 1) * LANES)
        for r0 in range(0, PROJ_T, CONV_ROWS):
            a = jnp.zeros((CONV_ROWS, LANES), F32)
            for j in range(CONV_W):
                a = a + cw_ref[j:j + 1, lanes] * buf[c, r0 + off + j:r0 + off + j + CONV_ROWS, :]
            acc[r0:r0 + CONV_ROWS, lanes] = a
    ycp_ref[...] = _ln_swish(acc[...] + cb_ref[...], lg_ref[...], lb_ref[...]).astype(BF16)


def _proj_specs():
    return [
        pl.BlockSpec((PROJ_T, D_MODEL), lambda i: (i, 0)),
        _resident((1, D_MODEL)),
        _resident((D_MODEL, IN_W)),
        _resident((1, N_BRANCH * D_MODEL)),
    ]


def _proj_conv(x, g1, w, bg, cw, cb, lg, lb, seq):
    t = x.shape[0]
    tiles = t // PROJ_T
    row = lambda w: pl.BlockSpec((PROJ_T, w), lambda i: (i, 0))
    return pl.pallas_call(
        functools.partial(_proj_conv_kernel, tiles_per_seq=seq // PROJ_T),
        grid=(tiles,),
        in_specs=_proj_specs() + [_resident((CONV_W, D_CONV)), _resident((1, D_CONV)),
                                  _resident((1, D_CONV)), _resident((1, D_CONV))],
        out_specs=[row(ATT_W), row(2 * KV_W), row(N_BRANCH * D_MODEL), row(D_CONV),
                   pl.BlockSpec((CONV_HALO, D_CONV), lambda i: (i, 0))],
        out_shape=[
            jax.ShapeDtypeStruct((t, ATT_W), BF16),
            jax.ShapeDtypeStruct((t, 2 * KV_W), F32),
            jax.ShapeDtypeStruct((t, N_BRANCH * D_MODEL), BF16),
            jax.ShapeDtypeStruct((t, D_CONV), BF16),
            jax.ShapeDtypeStruct((tiles * CONV_HALO, D_CONV), F32),
        ],
        scratch_shapes=[pltpu.VMEM((D_CONV // LANES, CONV_HALO + PROJ_T, LANES), F32),
                        pltpu.VMEM((PROJ_T, D_CONV), F32)],
        compiler_params=_params(("arbitrary",)),
        name="proj_conv",
    )(x, g1, w, bg, cw, cb, lg, lb)


def _proj(x, g1, w, bg):
    t = x.shape[0]
    row = lambda w: pl.BlockSpec((PROJ_T, w), lambda i: (i, 0))
    return pl.pallas_call(
        _proj_kernel,
        grid=(t // PROJ_T,),
        in_specs=_proj_specs(),
        out_specs=[row(ATT_W), row(2 * KV_W), row(D_CONV), row(N_BRANCH * D_MODEL)],
        out_shape=[
            jax.ShapeDtypeStruct((t, ATT_W), BF16),
            jax.ShapeDtypeStruct((t, 2 * KV_W), F32),
            jax.ShapeDtypeStruct((t, D_CONV), F32),
            jax.ShapeDtypeStruct((t, N_BRANCH * D_MODEL), BF16),
        ],
        compiler_params=_params(("arbitrary",)),
        name="proj",
    )(x, g1, w, bg)


def _sink_softmax(s, valid, sink):
    s = jnp.where(valid, s, NEG)
    m = jnp.maximum(jnp.max(s, axis=-1, keepdims=True), sink)
    e = jnp.exp(s - m)
    den = jnp.sum(e, axis=-1, keepdims=True) + jnp.exp(sink - m)
    return e, 1.0 / den


ATT_Q = 512
ATT_SUB = ATT_Q // WINDOW


def _prompt_attn_kernel(sink_ref, q_ref, kvc_ref, kvp_ref, o_ref, k_buf, v_buf):
    first_tile = pl.program_id(1) == 0
    k_buf[0:WINDOW, :] = kvp_ref[:, 0:KV_W].astype(BF16)
    v_buf[0:WINDOW, :] = kvp_ref[:, KV_W:2 * KV_W].astype(BF16)
    k_buf[WINDOW:WINDOW + ATT_Q, :] = kvc_ref[:, 0:KV_W].astype(BF16)
    v_buf[WINDOW:WINDOW + ATT_Q, :] = kvc_ref[:, KV_W:2 * KV_W].astype(BF16)

    qi = lax.broadcasted_iota(jnp.int32, (WINDOW, 2 * WINDOW), 0)
    kj = lax.broadcasted_iota(jnp.int32, (WINDOW, 2 * WINDOW), 1)
    band = (kj > qi) & (kj <= qi + WINDOW)
    head = lax.broadcasted_iota(jnp.int32, (N_HEADS, 1, 1), 0)
    sink = jnp.zeros((N_HEADS, 1, 1), F32)
    for hd in range(N_HEADS):
        sink = jnp.where(head == hd, sink_ref[hd], sink)

    def sub_block(sb, carry):
        row0 = pl.multiple_of(sb * WINDOW, WINDOW)
        lo = jnp.where(first_tile & (sb == 0), WINDOW, 0)
        valid = band & (kj >= lo)
        scores = []
        for h in range(N_KV_HEADS):
            k_h = k_buf[pl.ds(row0, 2 * WINDOW), h * HEAD_DIM:(h + 1) * HEAD_DIM]
            q_h = jnp.concatenate(
                [q_ref[pl.ds(row0, WINDOW), (h * GROUP + g) * HEAD_DIM:(h * GROUP + g + 1) * HEAD_DIM]
                 for g in range(GROUP)], axis=0)
            scores.append(_dot_nt(q_h, k_h).reshape(GROUP, WINDOW, 2 * WINDOW))
        s = jnp.concatenate(scores, axis=0)
        e, inv = _sink_softmax(s, valid[None], sink)
        e = e.astype(BF16)
        outs = []
        for h in range(N_KV_HEADS):
            v_h = v_buf[pl.ds(row0, 2 * WINDOW), h * HEAD_DIM:(h + 1) * HEAD_DIM]
            e_h = e[h * GROUP:(h + 1) * GROUP].reshape(GROUP * WINDOW, 2 * WINDOW)
            o_h = _dot(e_h, v_h) * inv[h * GROUP:(h + 1) * GROUP].reshape(GROUP * WINDOW, 1)
            outs.extend(o_h[g * WINDOW:(g + 1) * WINDOW, :] for g in range(GROUP))
        o_ref[pl.ds(row0, WINDOW), :] = jnp.concatenate(outs, axis=-1).astype(BF16)
        return carry

    lax.fori_loop(0, ATT_SUB, sub_block, 0)


def _prompt_attn(sink, q, kv, batch, seq):
    tiles = seq // ATT_Q
    prev_idx = lambda b, i: (jnp.maximum((b * tiles + i) * ATT_SUB - 1, 0), 0)
    return pl.pallas_call(
        _prompt_attn_kernel,
        grid=(batch, tiles),
        in_specs=[
            pl.BlockSpec(memory_space=pltpu.SMEM),
            pl.BlockSpec((ATT_Q, ATT_W), lambda b, i: (b * tiles + i, 0)),
            pl.BlockSpec((ATT_Q, 2 * KV_W), lambda b, i: (b * tiles + i, 0)),
            pl.BlockSpec((WINDOW, 2 * KV_W), prev_idx),
        ],
        out_specs=pl.BlockSpec((ATT_Q, ATT_W), lambda b, i: (b * tiles + i, 0)),
        out_shape=jax.ShapeDtypeStruct((batch * seq, ATT_W), BF16),
        scratch_shapes=[pltpu.VMEM((WINDOW + ATT_Q, KV_W), BF16),
                        pltpu.VMEM((WINDOW + ATT_Q, KV_W), BF16)],
        compiler_params=_params(("arbitrary", "arbitrary")),
        name="prompt_attn",
    )(sink, q, kv, kv)


DEC_G = 8
KEY_PAD = 8


def _sample_attn_kernel(sink_ref, q_ref, kv_ref, kvt_ref, ck_ref, cv_ref, o_ref, nk_ref, nv_ref,
                        *, dec_seq):
    rows = GROUP * dec_seq
    n_chain = DEC_G * N_KV_HEADS
    pad = jnp.zeros((KEY_PAD - dec_seq, KV_W), F32)

    scores, values = [], []
    for b in range(DEC_G):
        k_win, v_win = ck_ref[b], cv_ref[b]
        new = slice(b * dec_seq, (b + 1) * dec_seq)
        nk_ref[b] = jnp.concatenate([k_win[:, dec_seq:], kvt_ref[0:KV_W, new]], axis=1)
        nv_ref[b] = jnp.concatenate([v_win[:, dec_seq:], kvt_ref[KV_W:2 * KV_W, new]], axis=1)
        k_win, v_win = k_win.astype(BF16), v_win.astype(BF16)
        k_new = jnp.concatenate([kv_ref[b, :, 0:KV_W], pad], axis=0).astype(BF16)
        v_new = jnp.concatenate([kv_ref[b, :, KV_W:2 * KV_W], pad], axis=0).astype(BF16)
        for h in range(N_KV_HEADS):
            ch = slice(h * HEAD_DIM, (h + 1) * HEAD_DIM)
            q_h = q_ref[b, h]
            scores.append(jnp.concatenate(
                [_dot(q_h, k_win[ch, :]), _dot_nt(q_h, k_new[:, ch])], axis=1))
            values.append((v_win[ch, :], v_new[:, ch]))

    s = jnp.concatenate(scores, axis=0)
    r = lax.broadcasted_iota(jnp.int32, s.shape, 0)
    c = lax.broadcasted_iota(jnp.int32, s.shape, 1)
    t = r % dec_seq
    valid = ((c < WINDOW) & (c > t)) | ((c >= WINDOW) & (c - WINDOW <= t))
    head = (lax.broadcasted_iota(jnp.int32, (s.shape[0], 1), 0) // dec_seq) % N_HEADS
    sink_col = jnp.zeros((s.shape[0], 1), F32)
    for hd in range(N_HEADS):
        sink_col = jnp.where(head == hd, sink_ref[hd], sink_col)
    e, inv = _sink_softmax(s, valid, sink_col)
    e = e.astype(BF16)

    for i in range(n_chain):
        sl = slice(i * rows, (i + 1) * rows)
        v_win, v_new = values[i]
        o = _dot_nt(e[sl, 0:WINDOW], v_win) + _dot(e[sl, WINDOW:], v_new)
        o_ref[i // N_KV_HEADS, i % N_KV_HEADS] = o * inv[sl, :]


def _sample_attn(sink, q, kv, ck_t, cv_t):
    nb, dec_seq, _ = kv.shape
    rows = GROUP * dec_seq
    blk3 = lambda d1, d2: pl.BlockSpec((DEC_G, d1, d2), lambda i: (i, 0, 0))
    blk4 = pl.BlockSpec((DEC_G, N_KV_HEADS, rows, HEAD_DIM), lambda i: (i, 0, 0, 0))
    return pl.pallas_call(
        functools.partial(_sample_attn_kernel, dec_seq=dec_seq),
        grid=(nb // DEC_G,),
        in_specs=[
            pl.BlockSpec(memory_space=pltpu.SMEM),
            blk4, blk3(dec_seq, 2 * KV_W),
            pl.BlockSpec((None, 2 * KV_W, DEC_G * dec_seq), lambda i: (i, 0, 0)),
            blk3(KV_W, WINDOW), blk3(KV_W, WINDOW),
        ],
        out_specs=[blk4, blk3(KV_W, WINDOW), blk3(KV_W, WINDOW)],
        out_shape=[
            jax.ShapeDtypeStruct((nb, N_KV_HEADS, rows, HEAD_DIM), F32),
            jax.ShapeDtypeStruct((nb, KV_W, WINDOW), F32),
            jax.ShapeDtypeStruct((nb, KV_W, WINDOW), F32),
        ],
        compiler_params=_params(("arbitrary",)),
        name="sample_attn",
    )(sink, q, kv,
      jnp.transpose(kv.reshape(nb // DEC_G, DEC_G * dec_seq, 2 * KV_W), (0, 2, 1)), ck_t, cv_t)


def _to_head_major(q, nb, dec_seq):
    q = q.reshape(nb, dec_seq, N_KV_HEADS, GROUP, HEAD_DIM).transpose(0, 2, 3, 1, 4)
    return q.reshape(nb, N_KV_HEADS, GROUP * dec_seq, HEAD_DIM)


def _from_head_major(o, nb, dec_seq):
    o = o.reshape(nb, N_KV_HEADS, GROUP, dec_seq, HEAD_DIM).transpose(0, 3, 1, 2, 4)
    return o.reshape(nb * dec_seq, ATT_W)


SUBLANES = 8
CONV_G = 32


def _sample_conv_kernel(st_ref, u_ref, w_ref, cb_ref, g_ref, b_ref, o_ref, ns_ref, *, dec_seq):
    hist = CONV_W - 1
    ns_ref[0:hist - dec_seq] = st_ref[dec_seq:hist]
    ns_ref[hist - dec_seq:hist] = u_ref[...]
    for s0 in range(0, CONV_G, SUBLANES):
        seqs = slice(s0, s0 + SUBLANES)
        accs = [jnp.zeros((SUBLANES, D_CONV), F32) for _ in range(dec_seq)]
        for r in range(hist + dec_seq):
            row = st_ref[r, seqs, :] if r < hist else u_ref[r - hist, seqs, :]
            for t in range(dec_seq):
                j = r - t
                if 0 <= j < CONV_W:
                    accs[t] = accs[t] + w_ref[j:j + 1, :] * row
        for t in range(dec_seq):
            o_ref[t, seqs, :] = _ln_swish(accs[t] + cb_ref[...], g_ref[...], b_ref[...])


def _sample_conv(state_t, u_t, w, cb, g, b):
    dec_seq, nb, _ = u_t.shape
    hist = CONV_W - 1
    blk3 = lambda d0: pl.BlockSpec((d0, CONV_G, D_CONV), lambda i: (0, i, 0))
    return pl.pallas_call(
        functools.partial(_sample_conv_kernel, dec_seq=dec_seq),
        grid=(nb // CONV_G,),
        in_specs=[blk3(hist), blk3(dec_seq), _resident((CONV_W, D_CONV)),
                  _resident((1, D_CONV)), _resident((1, D_CONV)), _resident((1, D_CONV))],
        out_specs=[blk3(dec_seq), blk3(hist)],
        out_shape=[jax.ShapeDtypeStruct((dec_seq, nb, D_CONV), F32),
                   jax.ShapeDtypeStruct((hist, nb, D_CONV), F32)],
        compiler_params=_params(("arbitrary",)),
        name="sample_conv",
    )(state_t, u_t, w, cb, g, b)


MIX_T = 512
MIX_SUB = 256


def _mix_kernel(att_ref, ycp_ref, gate_ref, x_ref, wa_ref, wc_ref, wo_ref, g2_ref, h_ref, hn_ref):
    for r0 in range(0, MIX_T, MIX_SUB):
        rows = slice(r0, r0 + MIX_SUB)
        ya = _dot(att_ref[rows, :], wa_ref[...])
        yc = _dot(ycp_ref[rows, :], wc_ref[...])
        mixed = (gate_ref[rows, 0:D_MODEL].astype(F32) * ya
                 + gate_ref[rows, D_MODEL:2 * D_MODEL].astype(F32) * yc)
        h = x_ref[rows, :] + _dot(mixed.astype(BF16), wo_ref[...])
        h_ref[rows, :] = h
        hn_ref[rows, :] = _rms(h, g2_ref[...]).astype(BF16)


def _mix(att, ycp, gates, x, wa, wc, wo, g2):
    t = x.shape[0]
    row = lambda w: pl.BlockSpec((MIX_T, w), lambda i: (i, 0))
    return pl.pallas_call(
        _mix_kernel,
        grid=(t // MIX_T,),
        in_specs=[row(ATT_W), row(D_CONV), row(N_BRANCH * D_MODEL), row(D_MODEL),
                  _resident((ATT_W, D_MODEL)), _resident((D_CONV, D_MODEL)),
                  _resident((D_MODEL, D_MODEL)), _resident((1, D_MODEL))],
        out_specs=[row(D_MODEL), row(D_MODEL)],
        out_shape=[jax.ShapeDtypeStruct((t, D_MODEL), F32),
                   jax.ShapeDtypeStruct((t, D_MODEL), BF16)],
        compiler_params=_params(("arbitrary",)),
        name="mix",
    )(att, ycp, gates, x, wa, wc, wo, g2)


MLP_T = 512
MLP_F = 2048
MLP_CAST_F = 512


def _mlp_step(hn_ref, h_ref, wu_ref, wd_ref, gf_ref, y_ref, final_norm):
    f = pl.program_id(1)

    @pl.when(f == 0)
    def _():
        y_ref[...] = h_ref[...]

    z = jnp.maximum(_dot(hn_ref[...], wu_ref[...]), 0.0)
    y_ref[...] += _dot((z * z).astype(BF16), wd_ref[...])

    if final_norm:
        @pl.when(f == pl.num_programs(1) - 1)
        def _():
            y_ref[...] = _rms(y_ref[...], gf_ref[...])


def _mlp_kernel(hn_ref, h_ref, wu_ref, wd_ref, gf_ref, y_ref, *, final_norm):
    _mlp_step(hn_ref, h_ref, wu_ref, wd_ref, gf_ref, y_ref, final_norm)


def _mlp_cast_kernel(hn_ref, h_ref, wu_ref, wd_ref, gf_ref, y_ref, wub_ref, wdb_ref,
                     *, final_norm):
    wub_ref[...] = wu_ref[...].astype(BF16)
    wdb_ref[...] = wd_ref[...].astype(BF16)
    _mlp_step(hn_ref, h_ref, wub_ref, wdb_ref, gf_ref, y_ref, final_norm)


def _mlp_specs(chunk):
    return [
        pl.BlockSpec((MLP_T, D_MODEL), lambda i, f: (i, 0)),
        pl.BlockSpec((MLP_T, D_MODEL), lambda i, f: (i, 0)),
        pl.BlockSpec((D_MODEL, chunk), lambda i, f: (0, f)),
        pl.BlockSpec((chunk, D_MODEL), lambda i, f: (f, 0)),
        _resident((1, D_MODEL)),
    ]


def _mlp(hn, h, wu, wd, gf, final_norm):
    t = h.shape[0]
    return pl.pallas_call(
        functools.partial(_mlp_kernel, final_norm=final_norm),
        grid=(t // MLP_T, D_FF // MLP_F),
        in_specs=_mlp_specs(MLP_F),
        out_specs=pl.BlockSpec((MLP_T, D_MODEL), lambda i, f: (i, 0)),
        out_shape=jax.ShapeDtypeStruct((t, D_MODEL), F32),
        compiler_params=_params(("arbitrary", "arbitrary")),
        name="mlp",
    )(hn, h, wu, wd, gf)


def _mlp_cast(hn, h, wu32, wd32, gf, final_norm):
    assert h.shape[0] == MLP_T
    specs = _mlp_specs(MLP_CAST_F)
    return pl.pallas_call(
        functools.partial(_mlp_cast_kernel, final_norm=final_norm),
        grid=(1, D_FF // MLP_CAST_F),
        in_specs=specs,
        out_specs=[pl.BlockSpec((MLP_T, D_MODEL), lambda i, f: (i, 0)), specs[2], specs[3]],
        out_shape=[jax.ShapeDtypeStruct((MLP_T, D_MODEL), F32),
                   jax.ShapeDtypeStruct(wu32.shape, BF16),
                   jax.ShapeDtypeStruct(wd32.shape, BF16)],
        compiler_params=_params(("arbitrary", "arbitrary")),
        name="mlp_cast",
    )(hn, h, wu32, wd32, gf)


def kernel(x_prompt, x_sample, cache_k, cache_v, state_conv, norm1_g, w_in, b_gate, sink,
           w_attn_o, conv_w, conv_b, cln_g, cln_b, w_conv_o, w_out, norm2_g, w_up, w_down,
           norm_f_g):
    depth = w_in.shape[0]
    batch, seq, _ = x_prompt.shape
    nb, dec_seq, _ = x_sample.shape
    hist = CONV_W - 1
    hp = x_prompt.reshape(batch * seq, D_MODEL)
    hs = x_sample.reshape(nb * dec_seq, D_MODEL)
    row = lambda v: v.reshape(1, -1)
    gf = row(norm_f_g)
    outs = [[] for _ in range(6)]
    for l in range(depth):
        last = l == depth - 1
        wi, bg = w_in[l].astype(BF16), row(b_gate[l])
        wa, wc, wo = w_attn_o[l].astype(BF16), w_conv_o[l].astype(BF16), w_out[l].astype(BF16)
        g1, g2 = row(norm1_g[l]), row(norm2_g[l])
        cb, lg, lb = row(conv_b[l]), row(cln_g[l]), row(cln_b[l])

        q, kv, u, gates = _proj(hs, g1, wi, bg)
        att, nk, nv = _sample_attn(
            sink[l], _to_head_major(q, nb, dec_seq), kv.reshape(nb, dec_seq, 2 * KV_W),
            jnp.transpose(cache_k[l].reshape(nb, WINDOW, KV_W), (0, 2, 1)),
            jnp.transpose(cache_v[l].reshape(nb, WINDOW, KV_W), (0, 2, 1)))
        ycp_t, ns_t = _sample_conv(
            jnp.transpose(state_conv[l], (1, 0, 2)),
            jnp.transpose(u.reshape(nb, dec_seq, D_CONV), (1, 0, 2)), conv_w[l], cb, lg, lb)
        ycp = jnp.transpose(ycp_t, (1, 0, 2)).reshape(nb * dec_seq, D_CONV)
        h, hn = _mix(_from_head_major(att, nb, dec_seq).astype(BF16),
                     ycp.astype(BF16), gates, hs, wa, wc, wo, g2)
        if nb * dec_seq == MLP_T:
            hs, wu, wd = _mlp_cast(hn, h, w_up[l], w_down[l], gf, final_norm=last)
        else:
            wu, wd = w_up[l].astype(BF16), w_down[l].astype(BF16)
            hs = _mlp(hn, h, wu, wd, gf, final_norm=last)
        outs[3].append(jnp.transpose(nk, (0, 2, 1)).reshape(nb, WINDOW, N_KV_HEADS, HEAD_DIM))
        outs[4].append(jnp.transpose(nv, (0, 2, 1)).reshape(nb, WINDOW, N_KV_HEADS, HEAD_DIM))
        outs[5].append(jnp.transpose(ns_t, (1, 0, 2)))

        q, kv, gates, ycp, u_tail = _proj_conv(hp, g1, wi, bg, conv_w[l], cb, lg, lb, seq)
        att = _prompt_attn(sink[l], q, kv, batch, seq)
        h, hn = _mix(att, ycp, gates, hp, wa, wc, wo, g2)
        hp = _mlp(hn, h, wu, wd, gf, final_norm=last)
        kv3 = kv.reshape(batch, seq, 2 * KV_W)
        outs[0].append(kv3[:, seq - WINDOW:, :KV_W].reshape(batch, WINDOW, N_KV_HEADS, HEAD_DIM))
        outs[1].append(kv3[:, seq - WINDOW:, KV_W:].reshape(batch, WINDOW, N_KV_HEADS, HEAD_DIM))
        u_tail = u_tail.reshape(batch, seq // PROJ_T, CONV_HALO, D_CONV)
        outs[2].append(u_tail[:, -1, CONV_HALO - hist:, :])

    return (hp.reshape(batch, seq, D_MODEL), hs.reshape(nb, dec_seq, D_MODEL),
            *[jnp.stack(o) for o in outs])
```

```python
import functools

import jax
import jax.numpy as jnp
from jax import lax
from jax.experimental import pallas as pl
from jax.experimental.pallas import tpu as pltpu

D_MODEL = 2048
HEAD_DIM = 64
N_HEADS = 16
N_KV_HEADS = 4
GROUP = N_HEADS // N_KV_HEADS
ATT_W = N_HEADS * HEAD_DIM
KV_W = N_KV_HEADS * HEAD_DIM
WINDOW = 128
D_CONV = D_MODEL // 2
CONV_W = 31
N_BRANCH = 2
D_FF = 4 * D_MODEL
EPS = 1e-6
NEG = -1e30
ATTN_SCALE = HEAD_DIM ** -0.5

F32 = jnp.float32
BF16 = jnp.bfloat16

V7X_VMEM_LIMIT_BYTES = 58 * 1024 * 1024


def _dot(a, b):
    return jnp.dot(a, b, preferred_element_type=F32)


def _dot_nt(a, b):
    return lax.dot_general(a, b, (((1,), (1,)), ((), ())), preferred_element_type=F32)


def _sigmoid(x):
    return 1.0 / (1.0 + jnp.exp(-x))


def _rms(x, g):
    return x * lax.rsqrt(jnp.mean(x * x, axis=-1, keepdims=True) + EPS) * g


def _params(sem):
    return pltpu.CompilerParams(dimension_semantics=sem, vmem_limit_bytes=V7X_VMEM_LIMIT_BYTES)


def _resident(shape):
    return pl.BlockSpec(shape, lambda *_: (0,) * len(shape), pipeline_mode=pl.Buffered(1))


PROJ_T = 256
CONV_HALO = 32
CONV_ROWS = 64
LANES = 128


def _ln_swish(c, g, b):
    mu = jnp.mean(c, axis=-1, keepdims=True)
    xc = c - mu
    y = xc * lax.rsqrt(jnp.mean(xc * xc, axis=-1, keepdims=True) + EPS) * g + b
    return y * _sigmoid(y)


COL_K = ATT_W
COL_U = ATT_W + 2 * KV_W
COL_UB = COL_U + D_CONV
COL_G = COL_U + 2 * D_CONV
IN_W = COL_G + N_BRANCH * D_MODEL


def _project(x_ref, g1_ref, w_ref, bg_ref):
    xn = _rms(x_ref[...], g1_ref[...]).astype(BF16)
    q = (_dot(xn, w_ref[:, 0:COL_K]) * ATTN_SCALE).astype(BF16)
    kv = _dot(xn, w_ref[:, COL_K:COL_U])
    u = _dot(xn, w_ref[:, COL_U:COL_UB]) * _sigmoid(_dot(xn, w_ref[:, COL_UB:COL_G]))
    gates = _sigmoid(_dot(xn, w_ref[:, COL_G:IN_W]) + bg_ref[...]).astype(BF16)
    return q, kv, u, gates


def _proj_kernel(x_ref, g1_ref, w_ref, bg_ref, q_ref, kv_ref, u_ref, gate_ref):
    q_ref[...], kv_ref[...], u_ref[...], gate_ref[...] = _project(x_ref, g1_ref, w_ref, bg_ref)


def _proj_conv_kernel(x_ref, g1_ref, w_ref, bg_ref, cw_ref, cb_ref, lg_ref, lb_ref,
                      q_ref, kv_ref, gate_ref, ycp_ref, tail_ref, buf, acc, *, tiles_per_seq):
    first = pl.program_id(0) % tiles_per_seq == 0

    @pl.when(first)
    def _():
        buf[:, 0:CONV_HALO, :] = jnp.zeros((D_CONV // LANES, CONV_HALO, LANES), F32)

    @pl.when(jnp.logical_not(first))
    def _():
        buf[:, 0:CONV_HALO, :] = buf[:, PROJ_T:PROJ_T + CONV_HALO, :]

    q_ref[...], kv_ref[...], u, gate_ref[...] = _project(x_ref, g1_ref, w_ref, bg_ref)
    tail_ref[...] = u[PROJ_T - CONV_HALO:, :]
    for c in range(D_CONV // LANES):
        buf[c, CONV_HALO:CONV_HALO + PROJ_T, :] = u[:, c * LANES:(c + 1) * LANES]
    off = CONV_HALO - (CONV_W - 1)
    for c in range(D_CONV // LANES):
        lanes = slice(c * LANES, (c + 1) * LANES)
        for r0 in range(0, PROJ_T, CONV_ROWS):
            a = jnp.zeros((CONV_ROWS, LANES), F32)
            for j in range(CONV_W):
                a = a + cw_ref[j:j + 1, lanes] * buf[c, r0 + off + j:r0 + off + j + CONV_ROWS, :]
            acc[r0:r0 + CONV_ROWS, lanes] = a
    ycp_ref[...] = _ln_swish(acc[...] + cb_ref[...], lg_ref[...], lb_ref[...]).astype(BF16)


def _proj_specs():
    return [
        pl.BlockSpec((PROJ_T, D_MODEL), lambda i: (i, 0)),
        _resident((1, D_MODEL)),
        _resident((D_MODEL, IN_W)),
        _resident((1, N_BRANCH * D_MODEL)),
    ]


def _proj_conv(x, g1, w, bg, cw, cb, lg, lb, seq):
    t = x.shape[0]
    tiles = t // PROJ_T
    row = lambda w: pl.BlockSpec((PROJ_T, w), lambda i: (i, 0))
    return pl.pallas_call(
        functools.partial(_proj_conv_kernel, tiles_per_seq=seq // PROJ_T),
        grid=(tiles,),
        in_specs=_proj_specs() + [_resident((CONV_W, D_CONV)), _resident((1, D_CONV)),
                                  _resident((1, D_CONV)), _resident((1, D_CONV))],
        out_specs=[row(ATT_W), row(2 * KV_W), row(N_BRANCH * D_MODEL), row(D_CONV),
                   pl.BlockSpec((CONV_HALO, D_CONV), lambda i: (i, 0))],
        out_shape=[
            jax.ShapeDtypeStruct((t, ATT_W), BF16),
            jax.ShapeDtypeStruct((t, 2 * KV_W), F32),
            jax.ShapeDtypeStruct((t, N_BRANCH * D_MODEL), BF16),
            jax.ShapeDtypeStruct((t, D_CONV), BF16),
            jax.ShapeDtypeStruct((tiles * CONV_HALO, D_CONV), F32),
        ],
        scratch_shapes=[pltpu.VMEM((D_CONV // LANES, CONV_HALO + PROJ_T, LANES), F32),
                        pltpu.VMEM((PROJ_T, D_CONV), F32)],
        compiler_params=_params(("arbitrary",)),
        name="proj_conv",
    )(x, g1, w, bg, cw, cb, lg, lb)


PROJ_CAST_N = 512


def _proj_cast_kernel(x_ref, g1_ref, w_ref, bg_ref, q_ref, kv_ref, u_ref, gate_ref, wb_ref,
                      xn, p):
    j = pl.program_id(0)

    @pl.when(j == 0)
    def _():
        xn[...] = _rms(x_ref[...], g1_ref[...]).astype(BF16)

    wb_ref[...] = w_ref[...].astype(BF16)
    p[j] = _dot(xn[...], wb_ref[...])

    @pl.when(j == pl.num_programs(0) - 1)
    def _():
        n = PROJ_CAST_N
        for k in range(COL_K // n):
            q_ref[:, k * n:(k + 1) * n] = (p[k] * ATTN_SCALE).astype(BF16)
        for k in range((COL_U - COL_K) // n):
            kv_ref[:, k * n:(k + 1) * n] = p[COL_K // n + k]
        for k in range(D_CONV // n):
            u_ref[:, k * n:(k + 1) * n] = p[COL_U // n + k] * _sigmoid(p[COL_UB // n + k])
        for k in range((IN_W - COL_G) // n):
            cols = slice(k * n, (k + 1) * n)
            gate_ref[:, cols] = _sigmoid(p[COL_G // n + k] + bg_ref[:, cols]).astype(BF16)


def _proj_cast(x, g1, w32, bg):
    t = x.shape[0]
    full = lambda w: pl.BlockSpec((t, w), lambda j: (0, 0), pipeline_mode=pl.Buffered(1))
    return pl.pallas_call(
        _proj_cast_kernel,
        grid=(IN_W // PROJ_CAST_N,),
        in_specs=[full(D_MODEL), _resident((1, D_MODEL)),
                  pl.BlockSpec((D_MODEL, PROJ_CAST_N), lambda j: (0, j)),
                  _resident((1, N_BRANCH * D_MODEL))],
        out_specs=[full(ATT_W), full(2 * KV_W), full(D_CONV), full(N_BRANCH * D_MODEL),
                   pl.BlockSpec((D_MODEL, PROJ_CAST_N), lambda j: (0, j))],
        out_shape=[
            jax.ShapeDtypeStruct((t, ATT_W), BF16),
            jax.ShapeDtypeStruct((t, 2 * KV_W), F32),
            jax.ShapeDtypeStruct((t, D_CONV), F32),
            jax.ShapeDtypeStruct((t, N_BRANCH * D_MODEL), BF16),
            jax.ShapeDtypeStruct((D_MODEL, IN_W), BF16),
        ],
        scratch_shapes=[pltpu.VMEM((t, D_MODEL), BF16),
                        pltpu.VMEM((IN_W // PROJ_CAST_N, t, PROJ_CAST_N), F32)],
        compiler_params=_params(("arbitrary",)),
        name="proj_cast",
    )(x, g1, w32, bg)


def _proj(x, g1, w, bg):
    t = x.shape[0]
    row = lambda w: pl.BlockSpec((PROJ_T, w), lambda i: (i, 0))
    return pl.pallas_call(
        _proj_kernel,
        grid=(t // PROJ_T,),
        in_specs=_proj_specs(),
        out_specs=[row(ATT_W), row(2 * KV_W), row(D_CONV), row(N_BRANCH * D_MODEL)],
        out_shape=[
            jax.ShapeDtypeStruct((t, ATT_W), BF16),
            jax.ShapeDtypeStruct((t, 2 * KV_W), F32),
            jax.ShapeDtypeStruct((t, D_CONV), F32),
            jax.ShapeDtypeStruct((t, N_BRANCH * D_MODEL), BF16),
        ],
        compiler_params=_params(("arbitrary",)),
        name="proj",
    )(x, g1, w, bg)


def _sink_softmax(s, valid, sink):
    s = jnp.where(valid, s, NEG)
    m = jnp.maximum(jnp.max(s, axis=-1, keepdims=True), sink)
    e = jnp.exp(s - m)
    den = jnp.sum(e, axis=-1, keepdims=True) + jnp.exp(sink - m)
    return e, 1.0 / den


ATT_Q = 512
ATT_SUB = ATT_Q // WINDOW


def _prompt_attn_kernel(sink_ref, q_ref, kvc_ref, kvp_ref, o_ref, k_buf, v_buf):
    first_tile = pl.program_id(1) == 0
    k_buf[0:WINDOW, :] = kvp_ref[:, 0:KV_W].astype(BF16)
    v_buf[0:WINDOW, :] = kvp_ref[:, KV_W:2 * KV_W].astype(BF16)
    k_buf[WINDOW:WINDOW + ATT_Q, :] = kvc_ref[:, 0:KV_W].astype(BF16)
    v_buf[WINDOW:WINDOW + ATT_Q, :] = kvc_ref[:, KV_W:2 * KV_W].astype(BF16)

    qi = lax.broadcasted_iota(jnp.int32, (WINDOW, 2 * WINDOW), 0)
    kj = lax.broadcasted_iota(jnp.int32, (WINDOW, 2 * WINDOW), 1)
    band = (kj > qi) & (kj <= qi + WINDOW)
    head = lax.broadcasted_iota(jnp.int32, (N_HEADS, 1, 1), 0)
    sink = jnp.zeros((N_HEADS, 1, 1), F32)
    for hd in range(N_HEADS):
        sink = jnp.where(head == hd, sink_ref[hd], sink)

    def sub_block(sb, carry):
        row0 = pl.multiple_of(sb * WINDOW, WINDOW)
        lo = jnp.where(first_tile & (sb == 0), WINDOW, 0)
        valid = band & (kj >= lo)
        scores = []
        for h in range(N_KV_HEADS):
            k_h = k_buf[pl.ds(row0, 2 * WINDOW), h * HEAD_DIM:(h + 1) * HEAD_DIM]
            q_h = jnp.concatenate(
                [q_ref[pl.ds(row0, WINDOW), (h * GROUP + g) * HEAD_DIM:(h * GROUP + g + 1) * HEAD_DIM]
                 for g in range(GROUP)], axis=0)
            scores.append(_dot_nt(q_h, k_h).reshape(GROUP, WINDOW, 2 * WINDOW))
        s = jnp.concatenate(scores, axis=0)
        e, inv = _sink_softmax(s, valid[None], sink)
        e = e.astype(BF16)
        outs = []
        for h in range(N_KV_HEADS):
            v_h = v_buf[pl.ds(row0, 2 * WINDOW), h * HEAD_DIM:(h + 1) * HEAD_DIM]
            e_h = e[h * GROUP:(h + 1) * GROUP].reshape(GROUP * WINDOW, 2 * WINDOW)
            o_h = _dot(e_h, v_h) * inv[h * GROUP:(h + 1) * GROUP].reshape(GROUP * WINDOW, 1)
            outs.extend(o_h[g * WINDOW:(g + 1) * WINDOW, :] for g in range(GROUP))
        o_ref[pl.ds(row0, WINDOW), :] = jnp.concatenate(outs, axis=-1).astype(BF16)
        return carry

    lax.fori_loop(0, ATT_SUB, sub_block, 0)


def _prompt_attn(sink, q, kv, batch, seq):
    tiles = seq // ATT_Q
    prev_idx = lambda b, i: (jnp.maximum((b * tiles + i) * ATT_SUB - 1, 0), 0)
    return pl.pallas_call(
        _prompt_attn_kernel,
        grid=(batch, tiles),
        in_specs=[
            pl.BlockSpec(memory_space=pltpu.SMEM),
            pl.BlockSpec((ATT_Q, ATT_W), lambda b, i: (b * tiles + i, 0)),
            pl.BlockSpec((ATT_Q, 2 * KV_W), lambda b, i: (b * tiles + i, 0)),
            pl.BlockSpec((WINDOW, 2 * KV_W), prev_idx),
        ],
        out_specs=pl.BlockSpec((ATT_Q, ATT_W), lambda b, i: (b * tiles + i, 0)),
        out_shape=jax.ShapeDtypeStruct((batch * seq, ATT_W), BF16),
        scratch_shapes=[pltpu.VMEM((WINDOW + ATT_Q, KV_W), BF16),
                        pltpu.VMEM((WINDOW + ATT_Q, KV_W), BF16)],
        compiler_params=_params(("arbitrary", "arbitrary")),
        name="prompt_attn",
    )(sink, q, kv, kv)


DEC_G = 8
KEY_PAD = 8


def _sample_attn_kernel(sink_ref, q_ref, kv_ref, kvt_ref, ck_ref, cv_ref, o_ref, nk_ref, nv_ref,
                        *, dec_seq):
    rows = GROUP * dec_seq
    n_chain = DEC_G * N_KV_HEADS
    pad = jnp.zeros((KEY_PAD - dec_seq, KV_W), F32)

    scores, values = [], []
    for b in range(DEC_G):
        k_win, v_win = ck_ref[b], cv_ref[b]
        new = slice(b * dec_seq, (b + 1) * dec_seq)
        nk_ref[b] = jnp.concatenate([k_win[:, dec_seq:], kvt_ref[0:KV_W, new]], axis=1)
        nv_ref[b] = jnp.concatenate([v_win[:, dec_seq:], kvt_ref[KV_W:2 * KV_W, new]], axis=1)
        k_win, v_win = k_win.astype(BF16), v_win.astype(BF16)
        k_new = jnp.concatenate([kv_ref[b, :, 0:KV_W], pad], axis=0).astype(BF16)
        v_new = jnp.concatenate([kv_ref[b, :, KV_W:2 * KV_W], pad], axis=0).astype(BF16)
        for h in range(N_KV_HEADS):
            ch = slice(h * HEAD_DIM, (h + 1) * HEAD_DIM)
            q_h = q_ref[b, h]
            scores.append(jnp.concatenate(
                [_dot(q_h, k_win[ch, :]), _dot_nt(q_h, k_new[:, ch])], axis=1))
            values.append((v_win[ch, :], v_new[:, ch]))

    s = jnp.concatenate(scores, axis=0)
    r = lax.broadcasted_iota(jnp.int32, s.shape, 0)
    c = lax.broadcasted_iota(jnp.int32, s.shape, 1)
    t = r % dec_seq
    valid = ((c < WINDOW) & (c > t)) | ((c >= WINDOW) & (c - WINDOW <= t))
    head = (lax.broadcasted_iota(jnp.int32, (s.shape[0], 1), 0) // dec_seq) % N_HEADS
    sink_col = jnp.zeros((s.shape[0], 1), F32)
    for hd in range(N_HEADS):
        sink_col = jnp.where(head == hd, sink_ref[hd], sink_col)
    e, inv = _sink_softmax(s, valid, sink_col)
    e = e.astype(BF16)

    for i in range(n_chain):
        sl = slice(i * rows, (i + 1) * rows)
        v_win, v_new = values[i]
        o = _dot_nt(e[sl, 0:WINDOW], v_win) + _dot(e[sl, WINDOW:], v_new)
        o_ref[i // N_KV_HEADS, i % N_KV_HEADS] = o * inv[sl, :]


def _sample_attn(sink, q, kv, ck_t, cv_t):
    nb, dec_seq, _ = kv.shape
    rows = GROUP * dec_seq
    blk3 = lambda d1, d2: pl.BlockSpec((DEC_G, d1, d2), lambda i: (i, 0, 0))
    blk4 = pl.BlockSpec((DEC_G, N_KV_HEADS, rows, HEAD_DIM), lambda i: (i, 0, 0, 0))
    return pl.pallas_call(
        functools.partial(_sample_attn_kernel, dec_seq=dec_seq),
        grid=(nb // DEC_G,),
        in_specs=[
            pl.BlockSpec(memory_space=pltpu.SMEM),
            blk4, blk3(dec_seq, 2 * KV_W),
            pl.BlockSpec((None, 2 * KV_W, DEC_G * dec_seq), lambda i: (i, 0, 0)),
            blk3(KV_W, WINDOW), blk3(KV_W, WINDOW),
        ],
        out_specs=[blk4, blk3(KV_W, WINDOW), blk3(KV_W, WINDOW)],
        out_shape=[
            jax.ShapeDtypeStruct((nb, N_KV_HEADS, rows, HEAD_DIM), F32),
            jax.ShapeDtypeStruct((nb, KV_W, WINDOW), F32),
            jax.ShapeDtypeStruct((nb, KV_W, WINDOW), F32),
        ],
        compiler_params=_params(("arbitrary",)),
        name="sample_attn",
    )(sink, q, kv,
      jnp.transpose(kv.reshape(nb // DEC_G, DEC_G * dec_seq, 2 * KV_W), (0, 2, 1)), ck_t, cv_t)


def _to_head_major(q, nb, dec_seq):
    q = q.reshape(nb, dec_seq, N_KV_HEADS, GROUP, HEAD_DIM).transpose(0, 2, 3, 1, 4)
    return q.reshape(nb, N_KV_HEADS, GROUP * dec_seq, HEAD_DIM)


def _from_head_major(o, nb, dec_seq):
    o = o.reshape(nb, N_KV_HEADS, GROUP, dec_seq, HEAD_DIM).transpose(0, 3, 1, 2, 4)
    return o.reshape(nb * dec_seq, ATT_W)


SUBLANES = 8
CONV_G = 32


def _sample_conv_kernel(st_ref, u_ref, w_ref, cb_ref, g_ref, b_ref, o_ref, ns_ref, *, dec_seq):
    hist = CONV_W - 1
    ns_ref[0:hist - dec_seq] = st_ref[dec_seq:hist]
    ns_ref[hist - dec_seq:hist] = u_ref[...]
    for s0 in range(0, CONV_G, SUBLANES):
        seqs = slice(s0, s0 + SUBLANES)
        accs = [jnp.zeros((SUBLANES, D_CONV), F32) for _ in range(dec_seq)]
        for r in range(hist + dec_seq):
            row = st_ref[r, seqs, :] if r < hist else u_ref[r - hist, seqs, :]
            for t in range(dec_seq):
                j = r - t
                if 0 <= j < CONV_W:
                    accs[t] = accs[t] + w_ref[j:j + 1, :] * row
        for t in range(dec_seq):
            o_ref[t, seqs, :] = _ln_swish(accs[t] + cb_ref[...], g_ref[...], b_ref[...])


def _sample_conv(state_t, u_t, w, cb, g, b):
    dec_seq, nb, _ = u_t.shape
    hist = CONV_W - 1
    blk3 = lambda d0: pl.BlockSpec((d0, CONV_G, D_CONV), lambda i: (0, i, 0))
    return pl.pallas_call(
        functools.partial(_sample_conv_kernel, dec_seq=dec_seq),
        grid=(nb // CONV_G,),
        in_specs=[blk3(hist), blk3(dec_seq), _resident((CONV_W, D_CONV)),
                  _resident((1, D_CONV)), _resident((1, D_CONV)), _resident((1, D_CONV))],
        out_specs=[blk3(dec_seq), blk3(hist)],
        out_shape=[jax.ShapeDtypeStruct((dec_seq, nb, D_CONV), F32),
                   jax.ShapeDtypeStruct((hist, nb, D_CONV), F32)],
        compiler_params=_params(("arbitrary",)),
        name="sample_conv",
    )(state_t, u_t, w, cb, g, b)


MIX_T = 512
MIX_SUB = 256


def _mix_kernel(att_ref, ycp_ref, gate_ref, x_ref, wa_ref, wc_ref, wo_ref, g2_ref, h_ref, hn_ref):
    for r0 in range(0, MIX_T, MIX_SUB):
        rows = slice(r0, r0 + MIX_SUB)
        ya = _dot(att_ref[rows, :], wa_ref[...])
        yc = _dot(ycp_ref[rows, :], wc_ref[...])
        mixed = (gate_ref[rows, 0:D_MODEL].astype(F32) * ya
                 + gate_ref[rows, D_MODEL:2 * D_MODEL].astype(F32) * yc)
        h = x_ref[rows, :] + _dot(mixed.astype(BF16), wo_ref[...])
        h_ref[rows, :] = h
        hn_ref[rows, :] = _rms(h, g2_ref[...]).astype(BF16)


def _mix(att, ycp, gates, x, wa, wc, wo, g2):
    t = x.shape[0]
    row = lambda w: pl.BlockSpec((MIX_T, w), lambda i: (i, 0))
    return pl.pallas_call(
        _mix_kernel,
        grid=(t // MIX_T,),
        in_specs=[row(ATT_W), row(D_CONV), row(N_BRANCH * D_MODEL), row(D_MODEL),
                  _resident((ATT_W, D_MODEL)), _resident((D_CONV, D_MODEL)),
                  _resident((D_MODEL, D_MODEL)), _resident((1, D_MODEL))],
        out_specs=[row(D_MODEL), row(D_MODEL)],
        out_shape=[jax.ShapeDtypeStruct((t, D_MODEL), F32),
                   jax.ShapeDtypeStruct((t, D_MODEL), BF16)],
        compiler_params=_params(("arbitrary",)),
        name="mix",
    )(att, ycp, gates, x, wa, wc, wo, g2)


MLP_T = 512
MLP_F = 2048
MLP_CAST_F = 512


def _mlp_step(hn_ref, h_ref, wu_ref, wd_ref, gf_ref, y_ref, final_norm):
    f = pl.program_id(1)

    @pl.when(f == 0)
    def _():
        y_ref[...] = h_ref[...]

    z = jnp.maximum(_dot(hn_ref[...], wu_ref[...]), 0.0)
    y_ref[...] += _dot((z * z).astype(BF16), wd_ref[...])

    if final_norm:
        @pl.when(f == pl.num_programs(1) - 1)
        def _():
            y_ref[...] = _rms(y_ref[...], gf_ref[...])


def _mlp_kernel(hn_ref, h_ref, wu_ref, wd_ref, gf_ref, y_ref, *, final_norm):
    _mlp_step(hn_ref, h_ref, wu_ref, wd_ref, gf_ref, y_ref, final_norm)


def _mlp_cast_kernel(hn_ref, h_ref, wu_ref, wd_ref, gf_ref, y_ref, wub_ref, wdb_ref,
                     *, final_norm):
    wub_ref[...] = wu_ref[...].astype(BF16)
    wdb_ref[...] = wd_ref[...].astype(BF16)
    _mlp_step(hn_ref, h_ref, wub_ref, wdb_ref, gf_ref, y_ref, final_norm)


def _mlp_specs(chunk):
    return [
        pl.BlockSpec((MLP_T, D_MODEL), lambda i, f: (i, 0)),
        pl.BlockSpec((MLP_T, D_MODEL), lambda i, f: (i, 0)),
        pl.BlockSpec((D_MODEL, chunk), lambda i, f: (0, f)),
        pl.BlockSpec((chunk, D_MODEL), lambda i, f: (f, 0)),
        _resident((1, D_MODEL)),
    ]


def _mlp(hn, h, wu, wd, gf, final_norm):
    t = h.shape[0]
    return pl.pallas_call(
        functools.partial(_mlp_kernel, final_norm=final_norm),
        grid=(t // MLP_T, D_FF // MLP_F),
        in_specs=_mlp_specs(MLP_F),
        out_specs=pl.BlockSpec((MLP_T, D_MODEL), lambda i, f: (i, 0)),
        out_shape=jax.ShapeDtypeStruct((t, D_MODEL), F32),
        compiler_params=_params(("arbitrary", "arbitrary")),
        name="mlp",
    )(hn, h, wu, wd, gf)


def _mlp_cast(hn, h, wu32, wd32, gf, final_norm):
    assert h.shape[0] == MLP_T
    specs = _mlp_specs(MLP_CAST_F)
    return pl.pallas_call(
        functools.partial(_mlp_cast_kernel, final_norm=final_norm),
        grid=(1, D_FF // MLP_CAST_F),
        in_specs=specs,
        out_specs=[pl.BlockSpec((MLP_T, D_MODEL), lambda i, f: (i, 0)), specs[2], specs[3]],
        out_shape=[jax.ShapeDtypeStruct((MLP_T, D_MODEL), F32),
                   jax.ShapeDtypeStruct(wu32.shape, BF16),
                   jax.ShapeDtypeStruct(wd32.shape, BF16)],
        compiler_params=_params(("arbitrary", "arbitrary")),
        name="mlp_cast",
    )(hn, h, wu32, wd32, gf)


def kernel(x_prompt, x_sample, cache_k, cache_v, state_conv, norm1_g, w_in, b_gate, sink,
           w_attn_o, conv_w, conv_b, cln_g, cln_b, w_conv_o, w_out, norm2_g, w_up, w_down,
           norm_f_g):
    depth = w_in.shape[0]
    batch, seq, _ = x_prompt.shape
    nb, dec_seq, _ = x_sample.shape
    hist = CONV_W - 1
    hp = x_prompt.reshape(batch * seq, D_MODEL)
    hs = x_sample.reshape(nb * dec_seq, D_MODEL)
    row = lambda v: v.reshape(1, -1)
    gf = row(norm_f_g)
    outs = [[] for _ in range(6)]
    for l in range(depth):
        last = l == depth - 1
        bg = row(b_gate[l])
        wa, wc, wo = w_attn_o[l].astype(BF16), w_conv_o[l].astype(BF16), w_out[l].astype(BF16)
        g1, g2 = row(norm1_g[l]), row(norm2_g[l])
        cb, lg, lb = row(conv_b[l]), row(cln_g[l]), row(cln_b[l])

        if nb * dec_seq <= MLP_T:
            q, kv, u, gates, wi = _proj_cast(hs, g1, w_in[l], bg)
        else:
            wi = w_in[l].astype(BF16)
            q, kv, u, gates = _proj(hs, g1, wi, bg)
        att, nk, nv = _sample_attn(
            sink[l], _to_head_major(q, nb, dec_seq), kv.reshape(nb, dec_seq, 2 * KV_W),
            jnp.transpose(cache_k[l].reshape(nb, WINDOW, KV_W), (0, 2, 1)),
            jnp.transpose(cache_v[l].reshape(nb, WINDOW, KV_W), (0, 2, 1)))
        ycp_t, ns_t = _sample_conv(
            jnp.transpose(state_conv[l], (1, 0, 2)),
            jnp.transpose(u.reshape(nb, dec_seq, D_CONV), (1, 0, 2)), conv_w[l], cb, lg, lb)
        ycp = jnp.transpose(ycp_t, (1, 0, 2)).reshape(nb * dec_seq, D_CONV)
        h, hn = _mix(_from_head_major(att, nb, dec_seq).astype(BF16),
                     ycp.astype(BF16), gates, hs, wa, wc, wo, g2)
        if nb * dec_seq == MLP_T:
            hs, wu, wd = _mlp_cast(hn, h, w_up[l], w_down[l], gf, final_norm=last)
        else:
            wu, wd = w_up[l].astype(BF16), w_down[l].astype(BF16)
            hs = _mlp(hn, h, wu, wd, gf, final_norm=last)
        outs[3].append(jnp.transpose(nk, (0, 2, 1)).reshape(nb, WINDOW, N_KV_HEADS, HEAD_DIM))
        outs[4].append(jnp.transpose(nv, (0, 2, 1)).reshape(nb, WINDOW, N_KV_HEADS, HEAD_DIM))
        outs[5].append(jnp.transpose(ns_t, (1, 0, 2)))

        q, kv, gates, ycp, u_tail = _proj_conv(hp, g1, wi, bg, conv_w[l], cb, lg, lb, seq)
        att = _prompt_attn(sink[l], q, kv, batch, seq)
        h, hn = _mix(att, ycp, gates, hp, wa, wc, wo, g2)
        hp = _mlp(hn, h, wu, wd, gf, final_norm=last)
        kv3 = kv.reshape(batch, seq, 2 * KV_W)
        outs[0].append(kv3[:, seq - WINDOW:, :KV_W].reshape(batch, WINDOW, N_KV_HEADS, HEAD_DIM))
        outs[1].append(kv3[:, seq - WINDOW:, KV_W:].reshape(batch, WINDOW, N_KV_HEADS, HEAD_DIM))
        u_tail = u_tail.reshape(batch, seq // PROJ_T, CONV_HALO, D_CONV)
        outs[2].append(u_tail[:, -1, CONV_HALO - hist:, :])

    return (hp.reshape(batch, seq, D_MODEL), hs.reshape(nb, dec_seq, D_MODEL),
            *[jnp.stack(o) for o in outs])
```

```python
import functools

import jax
import jax.numpy as jnp
from jax import lax
from jax.experimental import pallas as pl
from jax.experimental.pallas import tpu as pltpu

D_MODEL = 2048
HEAD_DIM = 64
N_HEADS = 16
N_KV_HEADS = 4
GROUP = N_HEADS // N_KV_HEADS
ATT_W = N_HEADS * HEAD_DIM
KV_W = N_KV_HEADS * HEAD_DIM
WINDOW = 128
D_CONV = D_MODEL // 2
CONV_W = 31
N_BRANCH = 2
D_FF = 4 * D_MODEL
EPS = 1e-6
NEG = -1e30
ATTN_SCALE = HEAD_DIM ** -0.5

F32 = jnp.float32
BF16 = jnp.bfloat16

V7X_VMEM_LIMIT_BYTES = 58 * 1024 * 1024


def _dot(a, b):
    return jnp.dot(a, b, preferred_element_type=F32)


def _dot_nt(a, b):
    return lax.dot_general(a, b, (((1,), (1,)), ((), ())), preferred_element_type=F32)


def _sigmoid(x):
    return 1.0 / (1.0 + jnp.exp(-x))


def _rms(x, g):
    return x * lax.rsqrt(jnp.mean(x * x, axis=-1, keepdims=True) + EPS) * g


def _params(sem):
    return pltpu.CompilerParams(dimension_semantics=sem, vmem_limit_bytes=V7X_VMEM_LIMIT_BYTES)


def _resident(shape):
    return pl.BlockSpec(shape, lambda *_: (0,) * len(shape), pipeline_mode=pl.Buffered(1))


PROJ_T = 256
CONV_HALO = 32
CONV_ROWS = 64
LANES = 128


def _ln_swish(c, g, b):
    mu = jnp.mean(c, axis=-1, keepdims=True)
    xc = c - mu
    y = xc * lax.rsqrt(jnp.mean(xc * xc, axis=-1, keepdims=True) + EPS) * g + b
    return y * _sigmoid(y)


COL_K = ATT_W
COL_U = ATT_W + 2 * KV_W
COL_UB = COL_U + D_CONV
COL_G = COL_U + 2 * D_CONV
IN_W = COL_G + N_BRANCH * D_MODEL


def _project(x_ref, g1_ref, w_ref, bg_ref):
    xn = _rms(x_ref[...], g1_ref[...]).astype(BF16)
    q = (_dot(xn, w_ref[:, 0:COL_K]) * ATTN_SCALE).astype(BF16)
    kv = _dot(xn, w_ref[:, COL_K:COL_U])
    u = _dot(xn, w_ref[:, COL_U:COL_UB]) * _sigmoid(_dot(xn, w_ref[:, COL_UB:COL_G]))
    gates = _sigmoid(_dot(xn, w_ref[:, COL_G:IN_W]) + bg_ref[...]).astype(BF16)
    return q, kv, u, gates


def _proj_kernel(x_ref, g1_ref, w_ref, bg_ref, q_ref, kv_ref, u_ref, gate_ref):
    q_ref[...], kv_ref[...], u_ref[...], gate_ref[...] = _project(x_ref, g1_ref, w_ref, bg_ref)


def _proj_conv_kernel(x_ref, g1_ref, w_ref, bg_ref, cw_ref, cb_ref, lg_ref, lb_ref,
                      q_ref, kv_ref, gate_ref, ycp_ref, tail_ref, buf, acc, *, tiles_per_seq):
    first = pl.program_id(0) % tiles_per_seq == 0

    @pl.when(first)
    def _():
        buf[:, 0:CONV_HALO, :] = jnp.zeros((D_CONV // LANES, CONV_HALO, LANES), F32)

    @pl.when(jnp.logical_not(first))
    def _():
        buf[:, 0:CONV_HALO, :] = buf[:, PROJ_T:PROJ_T + CONV_HALO, :]

    q_ref[...], kv_ref[...], u, gate_ref[...] = _project(x_ref, g1_ref, w_ref, bg_ref)
    tail_ref[...] = u[PROJ_T - CONV_HALO:, :]
    for c in range(D_CONV // LANES):
        buf[c, CONV_HALO:CONV_HALO + PROJ_T, :] = u[:, c * LANES:(c + 1) * LANES]
    off = CONV_HALO - (CONV_W - 1)
    for c in range(D_CONV // LANES):
        lanes = slice(c * LANES, (c + 1) * LANES)
        for r0 in range(0, PROJ_T, CONV_ROWS):
            a = jnp.zeros((CONV_ROWS, LANES), F32)
            for j in range(CONV_W):
                a = a + cw_ref[j:j + 1, lanes] * buf[c, r0 + off + j:r0 + off + j + CONV_ROWS, :]
            acc[r0:r0 + CONV_ROWS, lanes] = a
    ycp_ref[...] = _ln_swish(acc[...] + cb_ref[...], lg_ref[...], lb_ref[...]).astype(BF16)


def _proj_specs():
    return [
        pl.BlockSpec((PROJ_T, D_MODEL), lambda i: (i, 0)),
        _resident((1, D_MODEL)),
        _resident((D_MODEL, IN_W)),
        _resident((1, N_BRANCH * D_MODEL)),
    ]


def _proj_conv(x, g1, w, bg, cw, cb, lg, lb, seq):
    t = x.shape[0]
    tiles = t // PROJ_T
    row = lambda w: pl.BlockSpec((PROJ_T, w), lambda i: (i, 0))
    return pl.pallas_call(
        functools.partial(_proj_conv_kernel, tiles_per_seq=seq // PROJ_T),
        grid=(tiles,),
        in_specs=_proj_specs() + [_resident((CONV_W, D_CONV)), _resident((1, D_CONV)),
                                  _resident((1, D_CONV)), _resident((1, D_CONV))],
        out_specs=[row(ATT_W), row(2 * KV_W), row(N_BRANCH * D_MODEL), row(D_CONV),
                   pl.BlockSpec((CONV_HALO, D_CONV), lambda i: (i, 0))],
        out_shape=[
            jax.ShapeDtypeStruct((t, ATT_W), BF16),
            jax.ShapeDtypeStruct((t, 2 * KV_W), F32),
            jax.ShapeDtypeStruct((t, N_BRANCH * D_MODEL), BF16),
            jax.ShapeDtypeStruct((t, D_CONV), BF16),
            jax.ShapeDtypeStruct((tiles * CONV_HALO, D_CONV), F32),
        ],
        scratch_shapes=[pltpu.VMEM((D_CONV // LANES, CONV_HALO + PROJ_T, LANES), F32),
                        pltpu.VMEM((PROJ_T, D_CONV), F32)],
        compiler_params=_params(("arbitrary",)),
        name="proj_conv",
    )(x, g1, w, bg, cw, cb, lg, lb)


PROJ_CAST_N = 512


def _proj_cast_kernel(x_ref, g1_ref, w_ref, bg_ref, q_ref, kv_ref, u_ref, gate_ref, wb_ref,
                      xn, p):
    j = pl.program_id(0)

    @pl.when(j == 0)
    def _():
        xn[...] = _rms(x_ref[...], g1_ref[...]).astype(BF16)

    wb_ref[...] = w_ref[...].astype(BF16)
    p[j] = _dot(xn[...], wb_ref[...])

    @pl.when(j == pl.num_programs(0) - 1)
    def _():
        n = PROJ_CAST_N
        for k in range(COL_K // n):
            q_ref[:, k * n:(k + 1) * n] = (p[k] * ATTN_SCALE).astype(BF16)
        for k in range((COL_U - COL_K) // n):
            kv_ref[:, k * n:(k + 1) * n] = p[COL_K // n + k]
        for k in range(D_CONV // n):
            u_ref[:, k * n:(k + 1) * n] = p[COL_U // n + k] * _sigmoid(p[COL_UB // n + k])
        for k in range((IN_W - COL_G) // n):
            cols = slice(k * n, (k + 1) * n)
            gate_ref[:, cols] = _sigmoid(p[COL_G // n + k] + bg_ref[:, cols]).astype(BF16)


def _proj_cast(x, g1, w32, bg):
    t = x.shape[0]
    full = lambda w: pl.BlockSpec((t, w), lambda j: (0, 0), pipeline_mode=pl.Buffered(1))
    return pl.pallas_call(
        _proj_cast_kernel,
        grid=(IN_W // PROJ_CAST_N,),
        in_specs=[full(D_MODEL), _resident((1, D_MODEL)),
                  pl.BlockSpec((D_MODEL, PROJ_CAST_N), lambda j: (0, j)),
                  _resident((1, N_BRANCH * D_MODEL))],
        out_specs=[full(ATT_W), full(2 * KV_W), full(D_CONV), full(N_BRANCH * D_MODEL),
                   pl.BlockSpec((D_MODEL, PROJ_CAST_N), lambda j: (0, j))],
        out_shape=[
            jax.ShapeDtypeStruct((t, ATT_W), BF16),
            jax.ShapeDtypeStruct((t, 2 * KV_W), F32),
            jax.ShapeDtypeStruct((t, D_CONV), F32),
            jax.ShapeDtypeStruct((t, N_BRANCH * D_MODEL), BF16),
            jax.ShapeDtypeStruct((D_MODEL, IN_W), BF16),
        ],
        scratch_shapes=[pltpu.VMEM((t, D_MODEL), BF16),
                        pltpu.VMEM((IN_W // PROJ_CAST_N, t, PROJ_CAST_N), F32)],
        compiler_params=_params(("arbitrary",)),
        name="proj_cast",
    )(x, g1, w32, bg)


def _proj(x, g1, w, bg):
    t = x.shape[0]
    row = lambda w: pl.BlockSpec((PROJ_T, w), lambda i: (i, 0))
    return pl.pallas_call(
        _proj_kernel,
        grid=(t // PROJ_T,),
        in_specs=_proj_specs(),
        out_specs=[row(ATT_W), row(2 * KV_W), row(D_CONV), row(N_BRANCH * D_MODEL)],
        out_shape=[
            jax.ShapeDtypeStruct((t, ATT_W), BF16),
            jax.ShapeDtypeStruct((t, 2 * KV_W), F32),
            jax.ShapeDtypeStruct((t, D_CONV), F32),
            jax.ShapeDtypeStruct((t, N_BRANCH * D_MODEL), BF16),
        ],
        compiler_params=_params(("arbitrary",)),
        name="proj",
    )(x, g1, w, bg)


def _sink_softmax(s, valid, sink):
    if valid is not None:
        s = jnp.where(valid, s, NEG)
    m = jnp.maximum(jnp.max(s, axis=-1, keepdims=True), sink)
    e = jnp.exp(s - m)
    den = jnp.sum(e, axis=-1, keepdims=True) + jnp.exp(sink - m)
    return e, 1.0 / den


ATT_Q = 1024
ATT_SUB = ATT_Q // WINDOW


def _prompt_attn_kernel(sink_ref, q_ref, kvc_ref, kvp_ref, o_ref, k_buf, v_buf):
    first_tile = pl.program_id(1) == 0
    k_buf[0:WINDOW, :] = kvp_ref[:, 0:KV_W].astype(BF16)
    v_buf[0:WINDOW, :] = kvp_ref[:, KV_W:2 * KV_W].astype(BF16)
    k_buf[WINDOW:WINDOW + ATT_Q, :] = kvc_ref[:, 0:KV_W].astype(BF16)
    v_buf[WINDOW:WINDOW + ATT_Q, :] = kvc_ref[:, KV_W:2 * KV_W].astype(BF16)

    qi = lax.broadcasted_iota(jnp.int32, (WINDOW, WINDOW), 0)
    cj = lax.broadcasted_iota(jnp.int32, (WINDOW, WINDOW), 1)
    from_prev = (cj > qi)[None]
    head = lax.broadcasted_iota(jnp.int32, (N_HEADS, 1, 1), 0)
    sink = jnp.zeros((N_HEADS, 1, 1), F32)
    for hd in range(N_HEADS):
        sink = jnp.where(head == hd, sink_ref[hd], sink)

    def sub_block(sb, carry):
        row0 = pl.multiple_of(sb * WINDOW, WINDOW)
        row1 = pl.multiple_of(row0 + WINDOW, WINDOW)
        no_prev = from_prev & (first_tile & (sb == 0))
        scores = []
        for h in range(N_KV_HEADS):
            k_h = k_buf[pl.ds(row0, 2 * WINDOW), h * HEAD_DIM:(h + 1) * HEAD_DIM]
            q_h = jnp.concatenate(
                [q_ref[pl.ds(row0, WINDOW), (h * GROUP + g) * HEAD_DIM:(h * GROUP + g + 1) * HEAD_DIM]
                 for g in range(GROUP)], axis=0)
            scores.append(_dot_nt(q_h, k_h).reshape(GROUP, WINDOW, 2 * WINDOW))
        s2 = jnp.concatenate(scores, axis=0)
        s = jnp.where(from_prev, s2[:, :, 0:WINDOW], s2[:, :, WINDOW:2 * WINDOW])
        s = jnp.where(no_prev, NEG, s)
        e, inv = _sink_softmax(s, None, sink)
        e_prev = jnp.where(from_prev, e, 0.0).astype(BF16)
        e_own = jnp.where(from_prev, 0.0, e).astype(BF16)
        outs = []
        for h in range(N_KV_HEADS):
            ch = slice(h * HEAD_DIM, (h + 1) * HEAD_DIM)
            hs = slice(h * GROUP, (h + 1) * GROUP)
            o_h = (_dot(e_prev[hs].reshape(GROUP * WINDOW, WINDOW), v_buf[pl.ds(row0, WINDOW), ch])
                   + _dot(e_own[hs].reshape(GROUP * WINDOW, WINDOW), v_buf[pl.ds(row1, WINDOW), ch]))
            o_h = o_h * inv[hs].reshape(GROUP * WINDOW, 1)
            outs.extend(o_h[g * WINDOW:(g + 1) * WINDOW, :] for g in range(GROUP))
        o_ref[pl.ds(row0, WINDOW), :] = jnp.concatenate(outs, axis=-1).astype(BF16)
        return carry

    lax.fori_loop(0, ATT_SUB, sub_block, 0)


def _prompt_attn(sink, q, kv, batch, seq):
    tiles = seq // ATT_Q
    prev_idx = lambda b, i: (jnp.maximum((b * tiles + i) * ATT_SUB - 1, 0), 0)
    return pl.pallas_call(
        _prompt_attn_kernel,
        grid=(batch, tiles),
        in_specs=[
            pl.BlockSpec(memory_space=pltpu.SMEM),
            pl.BlockSpec((ATT_Q, ATT_W), lambda b, i: (b * tiles + i, 0)),
            pl.BlockSpec((ATT_Q, 2 * KV_W), lambda b, i: (b * tiles + i, 0)),
            pl.BlockSpec((WINDOW, 2 * KV_W), prev_idx),
        ],
        out_specs=pl.BlockSpec((ATT_Q, ATT_W), lambda b, i: (b * tiles + i, 0)),
        out_shape=jax.ShapeDtypeStruct((batch * seq, ATT_W), BF16),
        scratch_shapes=[pltpu.VMEM((WINDOW + ATT_Q, KV_W), BF16),
                        pltpu.VMEM((WINDOW + ATT_Q, KV_W), BF16)],
        compiler_params=_params(("arbitrary", "arbitrary")),
        name="prompt_attn",
    )(sink, q, kv, kv)


DEC_G = 8
KEY_PAD = 8


def _sample_attn_kernel(sink_ref, q_ref, kv_ref, kvt_ref, ck_ref, cv_ref, o_ref, nk_ref, nv_ref,
                        *, dec_seq):
    rows = GROUP * dec_seq
    n_chain = DEC_G * N_KV_HEADS
    pad = jnp.zeros((KEY_PAD - dec_seq, KV_W), F32)

    scores, values = [], []
    for b in range(DEC_G):
        k_win, v_win = ck_ref[b], cv_ref[b]
        new = slice(b * dec_seq, (b + 1) * dec_seq)
        nk_ref[b] = jnp.concatenate([k_win[:, dec_seq:], kvt_ref[0:KV_W, new]], axis=1)
        nv_ref[b] = jnp.concatenate([v_win[:, dec_seq:], kvt_ref[KV_W:2 * KV_W, new]], axis=1)
        k_win, v_win = k_win.astype(BF16), v_win.astype(BF16)
        k_new = jnp.concatenate([kv_ref[b, :, 0:KV_W], pad], axis=0).astype(BF16)
        v_new = jnp.concatenate([kv_ref[b, :, KV_W:2 * KV_W], pad], axis=0).astype(BF16)
        for h in range(N_KV_HEADS):
            ch = slice(h * HEAD_DIM, (h + 1) * HEAD_DIM)
            q_h = q_ref[b, h]
            scores.append(jnp.concatenate(
                [_dot(q_h, k_win[ch, :]), _dot_nt(q_h, k_new[:, ch])], axis=1))
            values.append((v_win[ch, :], v_new[:, ch]))

    s = jnp.concatenate(scores, axis=0)
    r = lax.broadcasted_iota(jnp.int32, s.shape, 0)
    c = lax.broadcasted_iota(jnp.int32, s.shape, 1)
    t = r % dec_seq
    valid = ((c < WINDOW) & (c > t)) | ((c >= WINDOW) & (c - WINDOW <= t))
    head = (lax.broadcasted_iota(jnp.int32, (s.shape[0], 1), 0) // dec_seq) % N_HEADS
    sink_col = jnp.zeros((s.shape[0], 1), F32)
    for hd in range(N_HEADS):
        sink_col = jnp.where(head == hd, sink_ref[hd], sink_col)
    e, inv = _sink_softmax(s, valid, sink_col)
    e = e.astype(BF16)

    for i in range(n_chain):
        sl = slice(i * rows, (i + 1) * rows)
        v_win, v_new = values[i]
        o = _dot_nt(e[sl, 0:WINDOW], v_win) + _dot(e[sl, WINDOW:], v_new)
        o_ref[i // N_KV_HEADS, i % N_KV_HEADS] = o * inv[sl, :]


def _sample_attn(sink, q, kv, ck_t, cv_t):
    nb, dec_seq, _ = kv.shape
    rows = GROUP * dec_seq
    blk3 = lambda d1, d2: pl.BlockSpec((DEC_G, d1, d2), lambda i: (i, 0, 0))
    blk4 = pl.BlockSpec((DEC_G, N_KV_HEADS, rows, HEAD_DIM), lambda i: (i, 0, 0, 0))
    return pl.pallas_call(
        functools.partial(_sample_attn_kernel, dec_seq=dec_seq),
        grid=(nb // DEC_G,),
        in_specs=[
            pl.BlockSpec(memory_space=pltpu.SMEM),
            blk4, blk3(dec_seq, 2 * KV_W),
            pl.BlockSpec((None, 2 * KV_W, DEC_G * dec_seq), lambda i: (i, 0, 0)),
            blk3(KV_W, WINDOW), blk3(KV_W, WINDOW),
        ],
        out_specs=[blk4, blk3(KV_W, WINDOW), blk3(KV_W, WINDOW)],
        out_shape=[
            jax.ShapeDtypeStruct((nb, N_KV_HEADS, rows, HEAD_DIM), F32),
            jax.ShapeDtypeStruct((nb, KV_W, WINDOW), F32),
            jax.ShapeDtypeStruct((nb, KV_W, WINDOW), F32),
        ],
        compiler_params=_params(("arbitrary",)),
        name="sample_attn",
    )(sink, q, kv,
      jnp.transpose(kv.reshape(nb // DEC_G, DEC_G * dec_seq, 2 * KV_W), (0, 2, 1)), ck_t, cv_t)


def _to_head_major(q, nb, dec_seq):
    q = q.reshape(nb, dec_seq, N_KV_HEADS, GROUP, HEAD_DIM).transpose(0, 2, 3, 1, 4)
    return q.reshape(nb, N_KV_HEADS, GROUP * dec_seq, HEAD_DIM)


def _from_head_major(o, nb, dec_seq):
    o = o.reshape(nb, N_KV_HEADS, GROUP, dec_seq, HEAD_DIM).transpose(0, 3, 1, 2, 4)
    return o.reshape(nb * dec_seq, ATT_W)


SUBLANES = 8
CONV_G = 32


def _sample_conv_kernel(st_ref, u_ref, w_ref, cb_ref, g_ref, b_ref, o_ref, ns_ref, *, dec_seq):
    hist = CONV_W - 1
    ns_ref[0:hist - dec_seq] = st_ref[dec_seq:hist]
    ns_ref[hist - dec_seq:hist] = u_ref[...]
    for s0 in range(0, CONV_G, SUBLANES):
        seqs = slice(s0, s0 + SUBLANES)
        accs = [jnp.zeros((SUBLANES, D_CONV), F32) for _ in range(dec_seq)]
        for r in range(hist + dec_seq):
            row = st_ref[r, seqs, :] if r < hist else u_ref[r - hist, seqs, :]
            for t in range(dec_seq):
                j = r - t
                if 0 <= j < CONV_W:
                    accs[t] = accs[t] + w_ref[j:j + 1, :] * row
        for t in range(dec_seq):
            o_ref[t, seqs, :] = _ln_swish(accs[t] + cb_ref[...], g_ref[...], b_ref[...])


def _sample_conv(state_t, u_t, w, cb, g, b):
    dec_seq, nb, _ = u_t.shape
    hist = CONV_W - 1
    blk3 = lambda d0: pl.BlockSpec((d0, CONV_G, D_CONV), lambda i: (0, i, 0))
    return pl.pallas_call(
        functools.partial(_sample_conv_kernel, dec_seq=dec_seq),
        grid=(nb // CONV_G,),
        in_specs=[blk3(hist), blk3(dec_seq), _resident((CONV_W, D_CONV)),
                  _resident((1, D_CONV)), _resident((1, D_CONV)), _resident((1, D_CONV))],
        out_specs=[blk3(dec_seq), blk3(hist)],
        out_shape=[jax.ShapeDtypeStruct((dec_seq, nb, D_CONV), F32),
                   jax.ShapeDtypeStruct((hist, nb, D_CONV), F32)],
        compiler_params=_params(("arbitrary",)),
        name="sample_conv",
    )(state_t, u_t, w, cb, g, b)


MIX_T = 512
MIX_SUB = 256


def _mix_kernel(att_ref, ycp_ref, gate_ref, x_ref, wa_ref, wc_ref, wo_ref, g2_ref, h_ref, hn_ref):
    for r0 in range(0, MIX_T, MIX_SUB):
        rows = slice(r0, r0 + MIX_SUB)
        ya = _dot(att_ref[rows, :], wa_ref[...])
        yc = _dot(ycp_ref[rows, :], wc_ref[...])
        mixed = (gate_ref[rows, 0:D_MODEL].astype(F32) * ya
                 + gate_ref[rows, D_MODEL:2 * D_MODEL].astype(F32) * yc)
        h = x_ref[rows, :] + _dot(mixed.astype(BF16), wo_ref[...])
        h_ref[rows, :] = h
        hn_ref[rows, :] = _rms(h, g2_ref[...]).astype(BF16)


def _mix(att, ycp, gates, x, wa, wc, wo, g2):
    t = x.shape[0]
    row = lambda w: pl.BlockSpec((MIX_T, w), lambda i: (i, 0))
    return pl.pallas_call(
        _mix_kernel,
        grid=(t // MIX_T,),
        in_specs=[row(ATT_W), row(D_CONV), row(N_BRANCH * D_MODEL), row(D_MODEL),
                  _resident((ATT_W, D_MODEL)), _resident((D_CONV, D_MODEL)),
                  _resident((D_MODEL, D_MODEL)), _resident((1, D_MODEL))],
        out_specs=[row(D_MODEL), row(D_MODEL)],
        out_shape=[jax.ShapeDtypeStruct((t, D_MODEL), F32),
                   jax.ShapeDtypeStruct((t, D_MODEL), BF16)],
        compiler_params=_params(("arbitrary",)),
        name="mix",
    )(att, ycp, gates, x, wa, wc, wo, g2)


MLP_T = 512
MLP_F = 2048
MLP_CAST_F = 512


def _mlp_step(hn_ref, h_ref, wu_ref, wd_ref, gf_ref, y_ref, final_norm):
    f = pl.program_id(1)

    @pl.when(f == 0)
    def _():
        y_ref[...] = h_ref[...]

    z = jnp.maximum(_dot(hn_ref[...], wu_ref[...]), 0.0)
    y_ref[...] += _dot((z * z).astype(BF16), wd_ref[...])

    if final_norm:
        @pl.when(f == pl.num_programs(1) - 1)
        def _():
            y_ref[...] = _rms(y_ref[...], gf_ref[...])


def _mlp_kernel(hn_ref, h_ref, wu_ref, wd_ref, gf_ref, y_ref, *, final_norm):
    _mlp_step(hn_ref, h_ref, wu_ref, wd_ref, gf_ref, y_ref, final_norm)


def _mlp_cast_kernel(hn_ref, h_ref, wu_ref, wd_ref, gf_ref, y_ref, wub_ref, wdb_ref,
                     *, final_norm):
    wub_ref[...] = wu_ref[...].astype(BF16)
    wdb_ref[...] = wd_ref[...].astype(BF16)
    _mlp_step(hn_ref, h_ref, wub_ref, wdb_ref, gf_ref, y_ref, final_norm)


def _mlp_specs(chunk):
    return [
        pl.BlockSpec((MLP_T, D_MODEL), lambda i, f: (i, 0)),
        pl.BlockSpec((MLP_T, D_MODEL), lambda i, f: (i, 0)),
        pl.BlockSpec((D_MODEL, chunk), lambda i, f: (0, f)),
        pl.BlockSpec((chunk, D_MODEL), lambda i, f: (f, 0)),
        _resident((1, D_MODEL)),
    ]


def _mlp(hn, h, wu, wd, gf, final_norm):
    t = h.shape[0]
    return pl.pallas_call(
        functools.partial(_mlp_kernel, final_norm=final_norm),
        grid=(t // MLP_T, D_FF // MLP_F),
        in_specs=_mlp_specs(MLP_F),
        out_specs=pl.BlockSpec((MLP_T, D_MODEL), lambda i, f: (i, 0)),
        out_shape=jax.ShapeDtypeStruct((t, D_MODEL), F32),
        compiler_params=_params(("arbitrary", "arbitrary")),
        name="mlp",
    )(hn, h, wu, wd, gf)


def _mlp_cast(hn, h, wu32, wd32, gf, final_norm):
    assert h.shape[0] == MLP_T
    specs = _mlp_specs(MLP_CAST_F)
    return pl.pallas_call(
        functools.partial(_mlp_cast_kernel, final_norm=final_norm),
        grid=(1, D_FF // MLP_CAST_F),
        in_specs=specs,
        out_specs=[pl.BlockSpec((MLP_T, D_MODEL), lambda i, f: (i, 0)), specs[2], specs[3]],
        out_shape=[jax.ShapeDtypeStruct((MLP_T, D_MODEL), F32),
                   jax.ShapeDtypeStruct(wu32.shape, BF16),
                   jax.ShapeDtypeStruct(wd32.shape, BF16)],
        compiler_params=_params(("arbitrary", "arbitrary")),
        name="mlp_cast",
    )(hn, h, wu32, wd32, gf)


def kernel(x_prompt, x_sample, cache_k, cache_v, state_conv, norm1_g, w_in, b_gate, sink,
           w_attn_o, conv_w, conv_b, cln_g, cln_b, w_conv_o, w_out, norm2_g, w_up, w_down,
           norm_f_g):
    depth = w_in.shape[0]
    batch, seq, _ = x_prompt.shape
    nb, dec_seq, _ = x_sample.shape
    hist = CONV_W - 1
    hp = x_prompt.reshape(batch * seq, D_MODEL)
    hs = x_sample.reshape(nb * dec_seq, D_MODEL)
    row = lambda v: v.reshape(1, -1)
    gf = row(norm_f_g)
    outs = [[] for _ in range(6)]
    for l in range(depth):
        last = l == depth - 1
        bg = row(b_gate[l])
        wa, wc, wo = w_attn_o[l].astype(BF16), w_conv_o[l].astype(BF16), w_out[l].astype(BF16)
        g1, g2 = row(norm1_g[l]), row(norm2_g[l])
        cb, lg, lb = row(conv_b[l]), row(cln_g[l]), row(cln_b[l])

        if nb * dec_seq <= MLP_T:
            q, kv, u, gates, wi = _proj_cast(hs, g1, w_in[l], bg)
        else:
            wi = w_in[l].astype(BF16)
            q, kv, u, gates = _proj(hs, g1, wi, bg)
        att, nk, nv = _sample_attn(
            sink[l], _to_head_major(q, nb, dec_seq), kv.reshape(nb, dec_seq, 2 * KV_W),
            jnp.transpose(cache_k[l].reshape(nb, WINDOW, KV_W), (0, 2, 1)),
            jnp.transpose(cache_v[l].reshape(nb, WINDOW, KV_W), (0, 2, 1)))
        ycp_t, ns_t = _sample_conv(
            jnp.transpose(state_conv[l], (1, 0, 2)),
            jnp.transpose(u.reshape(nb, dec_seq, D_CONV), (1, 0, 2)), conv_w[l], cb, lg, lb)
        ycp = jnp.transpose(ycp_t, (1, 0, 2)).reshape(nb * dec_seq, D_CONV)
        h, hn = _mix(_from_head_major(att, nb, dec_seq).astype(BF16),
                     ycp.astype(BF16), gates, hs, wa, wc, wo, g2)
        if nb * dec_seq == MLP_T:
            hs, wu, wd = _mlp_cast(hn, h, w_up[l], w_down[l], gf, final_norm=last)
        else:
            wu, wd = w_up[l].astype(BF16), w_down[l].astype(BF16)
            hs = _mlp(hn, h, wu, wd, gf, final_norm=last)
        outs[3].append(jnp.transpose(nk, (0, 2, 1)).reshape(nb, WINDOW, N_KV_HEADS, HEAD_DIM))
        outs[4].append(jnp.transpose(nv, (0, 2, 1)).reshape(nb, WINDOW, N_KV_HEADS, HEAD_DIM))
        outs[5].append(jnp.transpose(ns_t, (1, 0, 2)))

        q, kv, gates, ycp, u_tail = _proj_conv(hp, g1, wi, bg, conv_w[l], cb, lg, lb, seq)
        att = _prompt_attn(sink[l], q, kv, batch, seq)
        h, hn = _mix(att, ycp, gates, hp, wa, wc, wo, g2)
        hp = _mlp(hn, h, wu, wd, gf, final_norm=last)
        kv3 = kv.reshape(batch, seq, 2 * KV_W)
        outs[0].append(kv3[:, seq - WINDOW:, :KV_W].reshape(batch, WINDOW, N_KV_HEADS, HEAD_DIM))
        outs[1].append(kv3[:, seq - WINDOW:, KV_W:].reshape(batch, WINDOW, N_KV_HEADS, HEAD_DIM))
        u_tail = u_tail.reshape(batch, seq // PROJ_T, CONV_HALO, D_CONV)
        outs[2].append(u_tail[:, -1, CONV_HALO - hist:, :])

    return (hp.reshape(batch, seq, D_MODEL), hs.reshape(nb, dec_seq, D_MODEL),
            *[jnp.stack(o) for o in outs])
```

```python
import functools

import jax
import jax.numpy as jnp
from jax import lax
from jax.experimental import pallas as pl
from jax.experimental.pallas import tpu as pltpu

D_MODEL = 2048
HEAD_DIM = 64
N_HEADS = 16
N_KV_HEADS = 4
GROUP = N_HEADS // N_KV_HEADS
ATT_W = N_HEADS * HEAD_DIM
KV_W = N_KV_HEADS * HEAD_DIM
WINDOW = 128
D_CONV = D_MODEL // 2
CONV_W = 31
N_BRANCH = 2
D_FF = 4 * D_MODEL
EPS = 1e-6
NEG = -1e30
ATTN_SCALE = HEAD_DIM ** -0.5

F32 = jnp.float32
BF16 = jnp.bfloat16

V7X_VMEM_LIMIT_BYTES = 58 * 1024 * 1024


def _dot(a, b):
    return jnp.dot(a, b, preferred_element_type=F32)


def _dot_nt(a, b):
    return lax.dot_general(a, b, (((1,), (1,)), ((), ())), preferred_element_type=F32)


def _sigmoid(x):
    return 1.0 / (1.0 + jnp.exp(-x))


def _rms(x, g):
    return x * lax.rsqrt(jnp.mean(x * x, axis=-1, keepdims=True) + EPS) * g


def _params(sem):
    return pltpu.CompilerParams(dimension_semantics=sem, vmem_limit_bytes=V7X_VMEM_LIMIT_BYTES)


def _resident(shape):
    return pl.BlockSpec(shape, lambda *_: (0,) * len(shape), pipeline_mode=pl.Buffered(1))


PROJ_T = 256
CONV_HALO = 32
CONV_ROWS = 64
LANES = 128


def _ln_swish(c, g, b):
    mu = jnp.mean(c, axis=-1, keepdims=True)
    xc = c - mu
    y = xc * lax.rsqrt(jnp.mean(xc * xc, axis=-1, keepdims=True) + EPS) * g + b
    return y * _sigmoid(y)


COL_K = ATT_W
COL_U = ATT_W + 2 * KV_W
COL_UB = COL_U + D_CONV
COL_G = COL_U + 2 * D_CONV
IN_W = COL_G + N_BRANCH * D_MODEL


def _project(x_ref, g1_ref, w_ref, bg_ref):
    xn = _rms(x_ref[...], g1_ref[...]).astype(BF16)
    q = (_dot(xn, w_ref[:, 0:COL_K]) * ATTN_SCALE).astype(BF16)
    kv = _dot(xn, w_ref[:, COL_K:COL_U])
    u = _dot(xn, w_ref[:, COL_U:COL_UB]) * _sigmoid(_dot(xn, w_ref[:, COL_UB:COL_G]))
    gates = _sigmoid(_dot(xn, w_ref[:, COL_G:IN_W]) + bg_ref[...]).astype(BF16)
    return q, kv, u, gates


def _proj_kernel(x_ref, g1_ref, w_ref, bg_ref, q_ref, kv_ref, u_ref, gate_ref):
    q_ref[...], kv_ref[...], u_ref[...], gate_ref[...] = _project(x_ref, g1_ref, w_ref, bg_ref)


def _proj_conv_kernel(x_ref, g1_ref, w_ref, bg_ref, cw_ref, cb_ref, lg_ref, lb_ref,
                      q_ref, kv_ref, gate_ref, ycp_ref, tail_ref, buf, acc, *, tiles_per_seq):
    first = pl.program_id(0) % tiles_per_seq == 0

    @pl.when(first)
    def _():
        buf[:, 0:CONV_HALO, :] = jnp.zeros((D_CONV // LANES, CONV_HALO, LANES), F32)

    @pl.when(jnp.logical_not(first))
    def _():
        buf[:, 0:CONV_HALO, :] = buf[:, PROJ_T:PROJ_T + CONV_HALO, :]

    q_ref[...], kv_ref[...], u, gate_ref[...] = _project(x_ref, g1_ref, w_ref, bg_ref)
    tail_ref[...] = u[PROJ_T - CONV_HALO:, :]
    for c in range(D_CONV // LANES):
        buf[c, CONV_HALO:CONV_HALO + PROJ_T, :] = u[:, c * LANES:(c + 1) * LANES]
    off = CONV_HALO - (CONV_W - 1)
    for c in range(D_CONV // LANES):
        lanes = slice(c * LANES, (c + 1) * LANES)
        for r0 in range(0, PROJ_T, CONV_ROWS):
            a = jnp.zeros((CONV_ROWS, LANES), F32)
            for j in range(CONV_W):
                a = a + cw_ref[j:j + 1, lanes] * buf[c, r0 + off + j:r0 + off + j + CONV_ROWS, :]
            acc[r0:r0 + CONV_ROWS, lanes] = a
    ycp_ref[...] = _ln_swish(acc[...] + cb_ref[...], lg_ref[...], lb_ref[...]).astype(BF16)


def _proj_specs():
    return [
        pl.BlockSpec((PROJ_T, D_MODEL), lambda i: (i, 0)),
        _resident((1, D_MODEL)),
        _resident((D_MODEL, IN_W)),
        _resident((1, N_BRANCH * D_MODEL)),
    ]


def _proj_conv(x, g1, w, bg, cw, cb, lg, lb, seq):
    t = x.shape[0]
    tiles = t // PROJ_T
    row = lambda w: pl.BlockSpec((PROJ_T, w), lambda i: (i, 0))
    return pl.pallas_call(
        functools.partial(_proj_conv_kernel, tiles_per_seq=seq // PROJ_T),
        grid=(tiles,),
        in_specs=_proj_specs() + [_resident((CONV_W, D_CONV)), _resident((1, D_CONV)),
                                  _resident((1, D_CONV)), _resident((1, D_CONV))],
        out_specs=[row(ATT_W), row(2 * KV_W), row(N_BRANCH * D_MODEL), row(D_CONV),
                   pl.BlockSpec((CONV_HALO, D_CONV), lambda i: (i, 0))],
        out_shape=[
            jax.ShapeDtypeStruct((t, ATT_W), BF16),
            jax.ShapeDtypeStruct((t, 2 * KV_W), F32),
            jax.ShapeDtypeStruct((t, N_BRANCH * D_MODEL), BF16),
            jax.ShapeDtypeStruct((t, D_CONV), BF16),
            jax.ShapeDtypeStruct((tiles * CONV_HALO, D_CONV), F32),
        ],
        scratch_shapes=[pltpu.VMEM((D_CONV // LANES, CONV_HALO + PROJ_T, LANES), F32),
                        pltpu.VMEM((PROJ_T, D_CONV), F32)],
        compiler_params=_params(("arbitrary",)),
        name="proj_conv",
    )(x, g1, w, bg, cw, cb, lg, lb)


PROJ_CAST_N = 512


def _proj_cast_kernel(x_ref, g1_ref, w_ref, bg_ref, q_ref, kv_ref, u_ref, gate_ref, wb_ref,
                      xn, p):
    j = pl.program_id(0)

    @pl.when(j == 0)
    def _():
        xn[...] = _rms(x_ref[...], g1_ref[...]).astype(BF16)

    wb_ref[...] = w_ref[...].astype(BF16)
    p[j] = _dot(xn[...], wb_ref[...])

    @pl.when(j == pl.num_programs(0) - 1)
    def _():
        n = PROJ_CAST_N
        for k in range(COL_K // n):
            q_ref[:, k * n:(k + 1) * n] = (p[k] * ATTN_SCALE).astype(BF16)
        for k in range((COL_U - COL_K) // n):
            kv_ref[:, k * n:(k + 1) * n] = p[COL_K // n + k]
        for k in range(D_CONV // n):
            u_ref[:, k * n:(k + 1) * n] = p[COL_U // n + k] * _sigmoid(p[COL_UB // n + k])
        for k in range((IN_W - COL_G) // n):
            cols = slice(k * n, (k + 1) * n)
            gate_ref[:, cols] = _sigmoid(p[COL_G // n + k] + bg_ref[:, cols]).astype(BF16)


def _proj_cast(x, g1, w32, bg):
    t = x.shape[0]
    full = lambda w: pl.BlockSpec((t, w), lambda j: (0, 0), pipeline_mode=pl.Buffered(1))
    return pl.pallas_call(
        _proj_cast_kernel,
        grid=(IN_W // PROJ_CAST_N,),
        in_specs=[full(D_MODEL), _resident((1, D_MODEL)),
                  pl.BlockSpec((D_MODEL, PROJ_CAST_N), lambda j: (0, j)),
                  _resident((1, N_BRANCH * D_MODEL))],
        out_specs=[full(ATT_W), full(2 * KV_W), full(D_CONV), full(N_BRANCH * D_MODEL),
                   pl.BlockSpec((D_MODEL, PROJ_CAST_N), lambda j: (0, j))],
        out_shape=[
            jax.ShapeDtypeStruct((t, ATT_W), BF16),
            jax.ShapeDtypeStruct((t, 2 * KV_W), F32),
            jax.ShapeDtypeStruct((t, D_CONV), F32),
            jax.ShapeDtypeStruct((t, N_BRANCH * D_MODEL), BF16),
            jax.ShapeDtypeStruct((D_MODEL, IN_W), BF16),
        ],
        scratch_shapes=[pltpu.VMEM((t, D_MODEL), BF16),
                        pltpu.VMEM((IN_W // PROJ_CAST_N, t, PROJ_CAST_N), F32)],
        compiler_params=_params(("arbitrary",)),
        name="proj_cast",
    )(x, g1, w32, bg)


def _proj(x, g1, w, bg):
    t = x.shape[0]
    row = lambda w: pl.BlockSpec((PROJ_T, w), lambda i: (i, 0))
    return pl.pallas_call(
        _proj_kernel,
        grid=(t // PROJ_T,),
        in_specs=_proj_specs(),
        out_specs=[row(ATT_W), row(2 * KV_W), row(D_CONV), row(N_BRANCH * D_MODEL)],
        out_shape=[
            jax.ShapeDtypeStruct((t, ATT_W), BF16),
            jax.ShapeDtypeStruct((t, 2 * KV_W), F32),
            jax.ShapeDtypeStruct((t, D_CONV), F32),
            jax.ShapeDtypeStruct((t, N_BRANCH * D_MODEL), BF16),
        ],
        compiler_params=_params(("arbitrary",)),
        name="proj",
    )(x, g1, w, bg)


def _sink_softmax(s, valid, sink):
    if valid is not None:
        s = jnp.where(valid, s, NEG)
    m = jnp.maximum(jnp.max(s, axis=-1, keepdims=True), sink)
    e = jnp.exp(s - m)
    den = jnp.sum(e, axis=-1, keepdims=True) + jnp.exp(sink - m)
    return e, 1.0 / den


ATT_Q = 1024
ATT_SUB = ATT_Q // WINDOW


def _prompt_attn_kernel(sink_ref, q_ref, kvc_ref, kvp_ref, o_ref, k_buf, v_buf):
    first_tile = pl.program_id(1) == 0

    def twice(v):
        return jnp.concatenate(
            [v[:, h * HEAD_DIM:(h + 1) * HEAD_DIM] for h in range(N_KV_HEADS) for _ in range(2)],
            axis=-1)

    k_buf[0:WINDOW, :] = kvp_ref[:, 0:KV_W].astype(BF16)
    v_buf[0:WINDOW, :] = twice(kvp_ref[:, KV_W:2 * KV_W].astype(BF16))
    k_buf[WINDOW:WINDOW + ATT_Q, :] = kvc_ref[:, 0:KV_W].astype(BF16)
    v_buf[WINDOW:WINDOW + ATT_Q, :] = twice(kvc_ref[:, KV_W:2 * KV_W].astype(BF16))
    low_half = lax.broadcasted_iota(jnp.int32, (WINDOW, 2 * HEAD_DIM), 1) < HEAD_DIM

    qi = lax.broadcasted_iota(jnp.int32, (WINDOW, WINDOW), 0)
    cj = lax.broadcasted_iota(jnp.int32, (WINDOW, WINDOW), 1)
    from_prev = (cj > qi)[None]
    head = lax.broadcasted_iota(jnp.int32, (N_HEADS, 1, 1), 0)
    sink = jnp.zeros((N_HEADS, 1, 1), F32)
    for hd in range(N_HEADS):
        sink = jnp.where(head == hd, sink_ref[hd], sink)

    def sub_block(sb, carry):
        row0 = pl.multiple_of(sb * WINDOW, WINDOW)
        row1 = pl.multiple_of(row0 + WINDOW, WINDOW)
        no_prev = from_prev & (first_tile & (sb == 0))
        scores = []
        for h in range(N_KV_HEADS):
            k_h = k_buf[pl.ds(row0, 2 * WINDOW), h * HEAD_DIM:(h + 1) * HEAD_DIM]
            q_h = jnp.concatenate(
                [q_ref[pl.ds(row0, WINDOW), (h * GROUP + g) * HEAD_DIM:(h * GROUP + g + 1) * HEAD_DIM]
                 for g in range(GROUP)], axis=0)
            scores.append(_dot_nt(q_h, k_h).reshape(GROUP, WINDOW, 2 * WINDOW))
        s2 = jnp.concatenate(scores, axis=0)
        s = jnp.where(from_prev, s2[:, :, 0:WINDOW], s2[:, :, WINDOW:2 * WINDOW])
        s = jnp.where(no_prev, NEG, s)
        e, inv = _sink_softmax(s, None, sink)
        e_prev = jnp.where(from_prev, e, 0.0).astype(BF16)
        e_own = jnp.where(from_prev, 0.0, e).astype(BF16)
        outs = []
        for h in range(N_KV_HEADS):
            ch = slice(2 * h * HEAD_DIM, 2 * (h + 1) * HEAD_DIM)
            hs = slice(h * GROUP, (h + 1) * GROUP)
            o_h = (_dot(e_prev[hs].reshape(GROUP * WINDOW, WINDOW), v_buf[pl.ds(row0, WINDOW), ch])
                   + _dot(e_own[hs].reshape(GROUP * WINDOW, WINDOW), v_buf[pl.ds(row1, WINDOW), ch]))
            o_h = o_h * inv[hs].reshape(GROUP * WINDOW, 1)
            for g in range(0, GROUP, 2):
                outs.append(jnp.where(low_half, o_h[g * WINDOW:(g + 1) * WINDOW, :],
                                      o_h[(g + 1) * WINDOW:(g + 2) * WINDOW, :]))
        o_ref[pl.ds(row0, WINDOW), :] = jnp.concatenate(outs, axis=-1).astype(BF16)
        return carry

    lax.fori_loop(0, ATT_SUB, sub_block, 0)


def _prompt_attn(sink, q, kv, batch, seq):
    tiles = seq // ATT_Q
    prev_idx = lambda b, i: (jnp.maximum((b * tiles + i) * ATT_SUB - 1, 0), 0)
    return pl.pallas_call(
        _prompt_attn_kernel,
        grid=(batch, tiles),
        in_specs=[
            pl.BlockSpec(memory_space=pltpu.SMEM),
            pl.BlockSpec((ATT_Q, ATT_W), lambda b, i: (b * tiles + i, 0)),
            pl.BlockSpec((ATT_Q, 2 * KV_W), lambda b, i: (b * tiles + i, 0)),
            pl.BlockSpec((WINDOW, 2 * KV_W), prev_idx),
        ],
        out_specs=pl.BlockSpec((ATT_Q, ATT_W), lambda b, i: (b * tiles + i, 0)),
        out_shape=jax.ShapeDtypeStruct((batch * seq, ATT_W), BF16),
        scratch_shapes=[pltpu.VMEM((WINDOW + ATT_Q, KV_W), BF16),
                        pltpu.VMEM((WINDOW + ATT_Q, 2 * KV_W), BF16)],
        compiler_params=_params(("arbitrary", "arbitrary")),
        name="prompt_attn",
    )(sink, q, kv, kv)


DEC_G = 8
KEY_PAD = 8


def _sample_attn_kernel(sink_ref, q_ref, kv_ref, kvt_ref, ck_ref, cv_ref, o_ref, nk_ref, nv_ref,
                        *, dec_seq):
    rows = GROUP * dec_seq
    n_chain = DEC_G * N_KV_HEADS
    pad = jnp.zeros((KEY_PAD - dec_seq, KV_W), F32)

    scores, values = [], []
    for b in range(DEC_G):
        k_win, v_win = ck_ref[b], cv_ref[b]
        new = slice(b * dec_seq, (b + 1) * dec_seq)
        nk_ref[b] = jnp.concatenate([k_win[:, dec_seq:], kvt_ref[0:KV_W, new]], axis=1)
        nv_ref[b] = jnp.concatenate([v_win[:, dec_seq:], kvt_ref[KV_W:2 * KV_W, new]], axis=1)
        k_win, v_win = k_win.astype(BF16), v_win.astype(BF16)
        k_new = jnp.concatenate([kv_ref[b, :, 0:KV_W], pad], axis=0).astype(BF16)
        v_new = jnp.concatenate([kv_ref[b, :, KV_W:2 * KV_W], pad], axis=0).astype(BF16)
        for h in range(N_KV_HEADS):
            ch = slice(h * HEAD_DIM, (h + 1) * HEAD_DIM)
            q_h = q_ref[b, h]
            scores.append(jnp.concatenate(
                [_dot(q_h, k_win[ch, :]), _dot_nt(q_h, k_new[:, ch])], axis=1))
            values.append((v_win[ch, :], v_new[:, ch]))

    s = jnp.concatenate(scores, axis=0)
    r = lax.broadcasted_iota(jnp.int32, s.shape, 0)
    c = lax.broadcasted_iota(jnp.int32, s.shape, 1)
    t = r % dec_seq
    valid = ((c < WINDOW) & (c > t)) | ((c >= WINDOW) & (c - WINDOW <= t))
    head = (lax.broadcasted_iota(jnp.int32, (s.shape[0], 1), 0) // dec_seq) % N_HEADS
    sink_col = jnp.zeros((s.shape[0], 1), F32)
    for hd in range(N_HEADS):
        sink_col = jnp.where(head == hd, sink_ref[hd], sink_col)
    e, inv = _sink_softmax(s, valid, sink_col)
    e = e.astype(BF16)

    for i in range(n_chain):
        sl = slice(i * rows, (i + 1) * rows)
        v_win, v_new = values[i]
        o = _dot_nt(e[sl, 0:WINDOW], v_win) + _dot(e[sl, WINDOW:], v_new)
        o_ref[i // N_KV_HEADS, i % N_KV_HEADS] = o * inv[sl, :]


def _sample_attn(sink, q, kv, ck_t, cv_t):
    nb, dec_seq, _ = kv.shape
    rows = GROUP * dec_seq
    blk3 = lambda d1, d2: pl.BlockSpec((DEC_G, d1, d2), lambda i: (i, 0, 0))
    blk4 = pl.BlockSpec((DEC_G, N_KV_HEADS, rows, HEAD_DIM), lambda i: (i, 0, 0, 0))
    return pl.pallas_call(
        functools.partial(_sample_attn_kernel, dec_seq=dec_seq),
        grid=(nb // DEC_G,),
        in_specs=[
            pl.BlockSpec(memory_space=pltpu.SMEM),
            blk4, blk3(dec_seq, 2 * KV_W),
            pl.BlockSpec((None, 2 * KV_W, DEC_G * dec_seq), lambda i: (i, 0, 0)),
            blk3(KV_W, WINDOW), blk3(KV_W, WINDOW),
        ],
        out_specs=[blk4, blk3(KV_W, WINDOW), blk3(KV_W, WINDOW)],
        out_shape=[
            jax.ShapeDtypeStruct((nb, N_KV_HEADS, rows, HEAD_DIM), F32),
            jax.ShapeDtypeStruct((nb, KV_W, WINDOW), F32),
            jax.ShapeDtypeStruct((nb, KV_W, WINDOW), F32),
        ],
        compiler_params=_params(("arbitrary",)),
        name="sample_attn",
    )(sink, q, kv,
      jnp.transpose(kv.reshape(nb // DEC_G, DEC_G * dec_seq, 2 * KV_W), (0, 2, 1)), ck_t, cv_t)


def _to_head_major(q, nb, dec_seq):
    q = q.reshape(nb, dec_seq, N_KV_HEADS, GROUP, HEAD_DIM).transpose(0, 2, 3, 1, 4)
    return q.reshape(nb, N_KV_HEADS, GROUP * dec_seq, HEAD_DIM)


def _from_head_major(o, nb, dec_seq):
    o = o.reshape(nb, N_KV_HEADS, GROUP, dec_seq, HEAD_DIM).transpose(0, 3, 1, 2, 4)
    return o.reshape(nb * dec_seq, ATT_W)


SUBLANES = 8
CONV_G = 32


def _sample_conv_kernel(st_ref, u_ref, w_ref, cb_ref, g_ref, b_ref, o_ref, ns_ref, *, dec_seq):
    hist = CONV_W - 1
    ns_ref[0:hist - dec_seq] = st_ref[dec_seq:hist]
    ns_ref[hist - dec_seq:hist] = u_ref[...]
    for s0 in range(0, CONV_G, SUBLANES):
        seqs = slice(s0, s0 + SUBLANES)
        accs = [jnp.zeros((SUBLANES, D_CONV), F32) for _ in range(dec_seq)]
        for r in range(hist + dec_seq):
            row = st_ref[r, seqs, :] if r < hist else u_ref[r - hist, seqs, :]
            for t in range(dec_seq):
                j = r - t
                if 0 <= j < CONV_W:
                    accs[t] = accs[t] + w_ref[j:j + 1, :] * row
        for t in range(dec_seq):
            o_ref[t, seqs, :] = _ln_swish(accs[t] + cb_ref[...], g_ref[...], b_ref[...])


def _sample_conv(state_t, u_t, w, cb, g, b):
    dec_seq, nb, _ = u_t.shape
    hist = CONV_W - 1
    blk3 = lambda d0: pl.BlockSpec((d0, CONV_G, D_CONV), lambda i: (0, i, 0))
    return pl.pallas_call(
        functools.partial(_sample_conv_kernel, dec_seq=dec_seq),
        grid=(nb // CONV_G,),
        in_specs=[blk3(hist), blk3(dec_seq), _resident((CONV_W, D_CONV)),
                  _resident((1, D_CONV)), _resident((1, D_CONV)), _resident((1, D_CONV))],
        out_specs=[blk3(dec_seq), blk3(hist)],
        out_shape=[jax.ShapeDtypeStruct((dec_seq, nb, D_CONV), F32),
                   jax.ShapeDtypeStruct((hist, nb, D_CONV), F32)],
        compiler_params=_params(("arbitrary",)),
        name="sample_conv",
    )(state_t, u_t, w, cb, g, b)


MIX_T = 512
MIX_SUB = 256


def _mix_kernel(att_ref, ycp_ref, gate_ref, x_ref, wa_ref, wc_ref, wo_ref, g2_ref, h_ref, hn_ref):
    for r0 in range(0, MIX_T, MIX_SUB):
        rows = slice(r0, r0 + MIX_SUB)
        ya = _dot(att_ref[rows, :], wa_ref[...])
        yc = _dot(ycp_ref[rows, :], wc_ref[...])
        mixed = (gate_ref[rows, 0:D_MODEL].astype(F32) * ya
                 + gate_ref[rows, D_MODEL:2 * D_MODEL].astype(F32) * yc)
        h = x_ref[rows, :] + _dot(mixed.astype(BF16), wo_ref[...])
        h_ref[rows, :] = h
        hn_ref[rows, :] = _rms(h, g2_ref[...]).astype(BF16)


def _mix(att, ycp, gates, x, wa, wc, wo, g2):
    t = x.shape[0]
    row = lambda w: pl.BlockSpec((MIX_T, w), lambda i: (i, 0))
    return pl.pallas_call(
        _mix_kernel,
        grid=(t // MIX_T,),
        in_specs=[row(ATT_W), row(D_CONV), row(N_BRANCH * D_MODEL), row(D_MODEL),
                  _resident((ATT_W, D_MODEL)), _resident((D_CONV, D_MODEL)),
                  _resident((D_MODEL, D_MODEL)), _resident((1, D_MODEL))],
        out_specs=[row(D_MODEL), row(D_MODEL)],
        out_shape=[jax.ShapeDtypeStruct((t, D_MODEL), F32),
                   jax.ShapeDtypeStruct((t, D_MODEL), BF16)],
        compiler_params=_params(("arbitrary",)),
        name="mix",
    )(att, ycp, gates, x, wa, wc, wo, g2)


MLP_T = 512
MLP_F = 2048
MLP_CAST_F = 512


def _mlp_step(hn_ref, h_ref, wu_ref, wd_ref, gf_ref, y_ref, final_norm):
    f = pl.program_id(1)

    @pl.when(f == 0)
    def _():
        y_ref[...] = h_ref[...]

    z = jnp.maximum(_dot(hn_ref[...], wu_ref[...]), 0.0)
    y_ref[...] += _dot((z * z).astype(BF16), wd_ref[...])

    if final_norm:
        @pl.when(f == pl.num_programs(1) - 1)
        def _():
            y_ref[...] = _rms(y_ref[...], gf_ref[...])


def _mlp_kernel(hn_ref, h_ref, wu_ref, wd_ref, gf_ref, y_ref, *, final_norm):
    _mlp_step(hn_ref, h_ref, wu_ref, wd_ref, gf_ref, y_ref, final_norm)


def _mlp_cast_kernel(hn_ref, h_ref, wu_ref, wd_ref, gf_ref, y_ref, wub_ref, wdb_ref,
                     *, final_norm):
    wub_ref[...] = wu_ref[...].astype(BF16)
    wdb_ref[...] = wd_ref[...].astype(BF16)
    _mlp_step(hn_ref, h_ref, wub_ref, wdb_ref, gf_ref, y_ref, final_norm)


def _mlp_specs(chunk):
    return [
        pl.BlockSpec((MLP_T, D_MODEL), lambda i, f: (i, 0)),
        pl.BlockSpec((MLP_T, D_MODEL), lambda i, f: (i, 0)),
        pl.BlockSpec((D_MODEL, chunk), lambda i, f: (0, f)),
        pl.BlockSpec((chunk, D_MODEL), lambda i, f: (f, 0)),
        _resident((1, D_MODEL)),
    ]


def _mlp(hn, h, wu, wd, gf, final_norm):
    t = h.shape[0]
    return pl.pallas_call(
        functools.partial(_mlp_kernel, final_norm=final_norm),
        grid=(t // MLP_T, D_FF // MLP_F),
        in_specs=_mlp_specs(MLP_F),
        out_specs=pl.BlockSpec((MLP_T, D_MODEL), lambda i, f: (i, 0)),
        out_shape=jax.ShapeDtypeStruct((t, D_MODEL), F32),
        compiler_params=_params(("arbitrary", "arbitrary")),
        name="mlp",
    )(hn, h, wu, wd, gf)


def _mlp_cast(hn, h, wu32, wd32, gf, final_norm):
    assert h.shape[0] == MLP_T
    specs = _mlp_specs(MLP_CAST_F)
    return pl.pallas_call(
        functools.partial(_mlp_cast_kernel, final_norm=final_norm),
        grid=(1, D_FF // MLP_CAST_F),
        in_specs=specs,
        out_specs=[pl.BlockSpec((MLP_T, D_MODEL), lambda i, f: (i, 0)), specs[2], specs[3]],
        out_shape=[jax.ShapeDtypeStruct((MLP_T, D_MODEL), F32),
                   jax.ShapeDtypeStruct(wu32.shape, BF16),
                   jax.ShapeDtypeStruct(wd32.shape, BF16)],
        compiler_params=_params(("arbitrary", "arbitrary")),
        name="mlp_cast",
    )(hn, h, wu32, wd32, gf)


def kernel(x_prompt, x_sample, cache_k, cache_v, state_conv, norm1_g, w_in, b_gate, sink,
           w_attn_o, conv_w, conv_b, cln_g, cln_b, w_conv_o, w_out, norm2_g, w_up, w_down,
           norm_f_g):
    depth = w_in.shape[0]
    batch, seq, _ = x_prompt.shape
    nb, dec_seq, _ = x_sample.shape
    hist = CONV_W - 1
    hp = x_prompt.reshape(batch * seq, D_MODEL)
    hs = x_sample.reshape(nb * dec_seq, D_MODEL)
    row = lambda v: v.reshape(1, -1)
    gf = row(norm_f_g)
    outs = [[] for _ in range(6)]
    for l in range(depth):
        last = l == depth - 1
        bg = row(b_gate[l])
        wa, wc, wo = w_attn_o[l].astype(BF16), w_conv_o[l].astype(BF16), w_out[l].astype(BF16)
        g1, g2 = row(norm1_g[l]), row(norm2_g[l])
        cb, lg, lb = row(conv_b[l]), row(cln_g[l]), row(cln_b[l])

        if nb * dec_seq <= MLP_T:
            q, kv, u, gates, wi = _proj_cast(hs, g1, w_in[l], bg)
        else:
            wi = w_in[l].astype(BF16)
            q, kv, u, gates = _proj(hs, g1, wi, bg)
        att, nk, nv = _sample_attn(
            sink[l], _to_head_major(q, nb, dec_seq), kv.reshape(nb, dec_seq, 2 * KV_W),
            jnp.transpose(cache_k[l].reshape(nb, WINDOW, KV_W), (0, 2, 1)),
            jnp.transpose(cache_v[l].reshape(nb, WINDOW, KV_W), (0, 2, 1)))
        ycp_t, ns_t = _sample_conv(
            jnp.transpose(state_conv[l], (1, 0, 2)),
            jnp.transpose(u.reshape(nb, dec_seq, D_CONV), (1, 0, 2)), conv_w[l], cb, lg, lb)
        ycp = jnp.transpose(ycp_t, (1, 0, 2)).reshape(nb * dec_seq, D_CONV)
        h, hn = _mix(_from_head_major(att, nb, dec_seq).astype(BF16),
                     ycp.astype(BF16), gates, hs, wa, wc, wo, g2)
        if nb * dec_seq == MLP_T:
            hs, wu, wd = _mlp_cast(hn, h, w_up[l], w_down[l], gf, final_norm=last)
        else:
            wu, wd = w_up[l].astype(BF16), w_down[l].astype(BF16)
            hs = _mlp(hn, h, wu, wd, gf, final_norm=last)
        outs[3].append(jnp.transpose(nk, (0, 2, 1)).reshape(nb, WINDOW, N_KV_HEADS, HEAD_DIM))
        outs[4].append(jnp.transpose(nv, (0, 2, 1)).reshape(nb, WINDOW, N_KV_HEADS, HEAD_DIM))
        outs[5].append(jnp.transpose(ns_t, (1, 0, 2)))

        q, kv, gates, ycp, u_tail = _proj_conv(hp, g1, wi, bg, conv_w[l], cb, lg, lb, seq)
        att = _prompt_attn(sink[l], q, kv, batch, seq)
        h, hn = _mix(att, ycp, gates, hp, wa, wc, wo, g2)
        hp = _mlp(hn, h, wu, wd, gf, final_norm=last)
        kv3 = kv.reshape(batch, seq, 2 * KV_W)
        outs[0].append(kv3[:, seq - WINDOW:, :KV_W].reshape(batch, WINDOW, N_KV_HEADS, HEAD_DIM))
        outs[1].append(kv3[:, seq - WINDOW:, KV_W:].reshape(batch, WINDOW, N_KV_HEADS, HEAD_DIM))
        u_tail = u_tail.reshape(batch, seq // PROJ_T, CONV_HALO, D_CONV)
        outs[2].append(u_tail[:, -1, CONV_HALO - hist:, :])

    return (hp.reshape(batch, seq, D_MODEL), hs.reshape(nb, dec_seq, D_MODEL),
            *[jnp.stack(o) for o in outs])
```
